```python
import jax, jax.numpy as jnp
from jax import lax
import numpy as np

D_MODEL = 4096
BATCH = 8
SEQ = 2048
DEPTH = 1
DEC_BATCH = 32
DEC_SEQ = 32
PAST_LEN = 1024

CHUNK = 64
N_LEFT_CHUNKS = 8
LEFT_REACH = N_LEFT_CHUNKS * CHUNK
BAND = (N_LEFT_CHUNKS + 1) * CHUNK
D_ATTN = D_MODEL // 2
D_CONV = D_MODEL - D_ATTN
HEAD_DIM = 128
N_HEADS = D_ATTN // HEAD_DIM
MAX_REL = 128
N_REL = 2 * MAX_REL + 1
CONV_WIDTH = 31
D_IN = 3 * D_ATTN + 2 * D_CONV
N_EXPERTS = 64
TOP_K = 8
N_GROUPS = 8
TOPK_GROUPS = 4
D_EXPERT = 1024
D_SHARED = 1024
ROUTED_SCALE = 2.5
EPS = 1e-6
ATTN_SCALE = HEAD_DIM ** -0.5
NEG_INF = -1e30

kernel_name = 'hybrid_streaming_band_attn_conformer_conv_moe_step'


def rmsnorm(x, g):
    xf = x.astype(jnp.float32)
    y = xf * lax.rsqrt(jnp.mean(xf * xf, axis=-1, keepdims=True) + EPS)
    return (y * g.astype(jnp.float32)).astype(x.dtype)


def layernorm(x, g, b):
    xf = x.astype(jnp.float32)
    xc = xf - jnp.mean(xf, axis=-1, keepdims=True)
    var = jnp.mean(xc * xc, axis=-1, keepdims=True)
    return (xc * lax.rsqrt(var + EPS) * g.astype(jnp.float32) + b.astype(jnp.float32)).astype(x.dtype)


def adaln(c, w_ada, b_ada):
    mod = jax.nn.silu(c) @ w_ada + b_ada
    return jnp.split(mod[:, None, :], 6, axis=-1)


def modulate(h, shift, scale):
    return h * (1 + scale) + shift


def to_heads(t):
    b, s, _ = t.shape
    return t.reshape(b, s, N_HEADS, HEAD_DIM).transpose(0, 2, 1, 3)


def band_attention(q, k, v, rel_bias, q_pos, k_pos):
    s = jnp.einsum('bhqd,bhkd->bhqk', q, k, preferred_element_type=jnp.float32) * ATTN_SCALE
    rel = jnp.clip(k_pos[None, :] - q_pos[:, None], -MAX_REL, MAX_REL) + MAX_REL
    s = s + rel_bias.astype(jnp.float32)[:, rel][None]
    s = jnp.where((k_pos >= 0)[None, None, None, :], s, NEG_INF)
    p = jax.nn.softmax(s, axis=-1).astype(v.dtype)
    return jnp.einsum('bhqk,bhkd->bhqd', p, v)


def prompt_band_attention(q, k, v, rel_bias):
    b, h, s, d = q.shape
    n_chunks = s // CHUNK
    pad = ((0, 0), (0, 0), (LEFT_REACH, 0), (0, 0))
    kp = jnp.pad(k, pad)
    vp = jnp.pad(v, pad)

    def one_chunk(ci):
        qb = lax.dynamic_slice_in_dim(q, ci * CHUNK, CHUNK, axis=2)
        kb = lax.dynamic_slice_in_dim(kp, ci * CHUNK, BAND, axis=2)
        vb = lax.dynamic_slice_in_dim(vp, ci * CHUNK, BAND, axis=2)
        q_pos = ci * CHUNK + jnp.arange(CHUNK)
        k_pos = (ci - N_LEFT_CHUNKS) * CHUNK + jnp.arange(BAND)
        return band_attention(qb, kb, vb, rel_bias, q_pos, k_pos)

    out = lax.map(one_chunk, jnp.arange(n_chunks))
    return out.transpose(1, 2, 0, 3, 4).reshape(b, h, s, d)


def sample_band_attention(q, k_new, v_new, cache_k, cache_v, rel_bias):
    past_win = cache_k.shape[2]
    t = q.shape[2]
    k = jnp.concatenate([cache_k, k_new], axis=2)
    v = jnp.concatenate([cache_v, v_new], axis=2)
    q_pos = PAST_LEN + jnp.arange(t)
    k_pos = PAST_LEN - past_win + jnp.arange(past_win + t)
    return band_attention(q, k, v, rel_bias, q_pos, k_pos)


def conformer_conv(u, prefix, conv_w, conv_b, ln_g, ln_b):
    a, g = jnp.split(u, 2, axis=-1)
    xg = a * jax.nn.sigmoid(g)
    xp = jnp.concatenate([prefix, xg], axis=1)
    y = lax.conv_general_dilated(xp, conv_w[:, None, :], window_strides=(1,), padding='VALID',
                                 dimension_numbers=('NWC', 'WIO', 'NWC'),
                                 feature_group_count=D_CONV) + conv_b
    y = jax.nn.silu(layernorm(y, ln_g, ln_b))
    return y, xp[:, -(CONV_WIDTH - 1):]


def mixer_sublayer(x, shift, scale, gate, cache_k, cache_v, conv_prefix,
                   g_mix, w_in, g_q, g_k, rel_bias, conv_w, conv_b, conv_ln_g, conv_ln_b,
                   beta_attn, beta_conv, w_out):
    b, t, _ = x.shape
    h = modulate(rmsnorm(x, g_mix), shift, scale)
    z = h @ w_in
    q, k, v, u = jnp.split(z, [D_ATTN, 2 * D_ATTN, 3 * D_ATTN], axis=-1)
    q = rmsnorm(to_heads(q), g_q)
    k = rmsnorm(to_heads(k), g_k)
    v = to_heads(v)
    if cache_k is None:
        o = prompt_band_attention(q, k, v, rel_bias)
        keep = min(LEFT_REACH, t)
        k_keep, v_keep = k[:, :, t - keep:], v[:, :, t - keep:]
        prefix = jnp.zeros((b, CONV_WIDTH - 1, D_CONV), x.dtype)
    else:
        o = sample_band_attention(q, k, v, cache_k, cache_v, rel_bias)
        k_keep, v_keep = k, v
        prefix = conv_prefix
    o_attn = o.transpose(0, 2, 1, 3).reshape(b, t, D_ATTN)
    o_conv, conv_state = conformer_conv(u, prefix, conv_w, conv_b, conv_ln_g, conv_ln_b)
    mixed = jnp.concatenate([rmsnorm(o_attn, beta_attn), rmsnorm(o_conv, beta_conv)], axis=-1)
    return x + gate * (mixed @ w_out), k_keep, v_keep, conv_state


def swiglu(x, wg, wu, wd):
    return (jax.nn.silu(x @ wg) * (x @ wu)) @ wd


def moe(h, w_router, router_bias, w_gate, w_up, w_down, ws_gate, ws_up, ws_down):
    b, t, d = h.shape
    xt = h.reshape(b * t, d)
    scores = jax.nn.sigmoid((xt @ w_router).astype(jnp.float32))
    choice = scores + router_bias.astype(jnp.float32)
    grp = choice.reshape(-1, N_GROUPS, N_EXPERTS // N_GROUPS)
    grp_score = jnp.sum(lax.top_k(grp, 2)[0], axis=-1)
    _, gidx = lax.top_k(grp_score, TOPK_GROUPS)
    gmask = jnp.sum(jax.nn.one_hot(gidx, N_GROUPS, dtype=jnp.float32), axis=1)
    emask = jnp.repeat(gmask, N_EXPERTS // N_GROUPS, axis=1)
    _, eidx = lax.top_k(jnp.where(emask > 0, choice, NEG_INF), TOP_K)
    w = jnp.take_along_axis(scores, eidx, axis=1)
    w = w / jnp.sum(w, axis=-1, keepdims=True) * ROUTED_SCALE
    gates = jnp.sum(jax.nn.one_hot(eidx, N_EXPERTS, dtype=jnp.float32) * w[..., None], axis=1)
    gates = gates.astype(h.dtype)
    y = swiglu(xt, ws_gate, ws_up, ws_down)
    for e in range(N_EXPERTS):
        y = y + gates[:, e:e + 1] * swiglu(xt, w_gate[e], w_up[e], w_down[e])
    return y.reshape(b, t, d)


def ffn_sublayer(x, shift, scale, gate, g_ffn, w_router, router_bias, w_gate, w_up, w_down,
                 ws_gate, ws_up, ws_down):
    h = modulate(rmsnorm(x, g_ffn), shift, scale)
    return x + gate * moe(h, w_router, router_bias, w_gate, w_up, w_down, ws_gate, ws_up, ws_down)


def setup_inputs(seed: int = 0) -> dict:
    key = jax.random.key(seed)
    ks = iter(jax.random.split(key, 40))

    def nrm(shape, s):
        return jax.random.normal(next(ks), shape, jnp.float32) * s

    past_win = min(LEFT_REACH, PAST_LEN)
    L = DEPTH
    return {
        'x_prompt': nrm((BATCH, SEQ, D_MODEL), 1.0),
        'x_sample': nrm((DEC_BATCH, DEC_SEQ, D_MODEL), 1.0),
        'c_prompt': nrm((BATCH, D_MODEL), 1.0),
        'c_sample': nrm((DEC_BATCH, D_MODEL), 1.0),
        'cache_k': nrm((L, DEC_BATCH, N_HEADS, past_win, HEAD_DIM), 1.0),
        'cache_v': nrm((L, DEC_BATCH, N_HEADS, past_win, HEAD_DIM), 1.0),
        'cache_conv': nrm((L, DEC_BATCH, CONV_WIDTH - 1, D_CONV), 0.5),
        'w_ada': nrm((L, D_MODEL, 6 * D_MODEL), 0.5 * D_MODEL ** -0.5),
        'b_ada': nrm((L, 6 * D_MODEL), 0.02),
        'g_mix': 1.0 + nrm((L, D_MODEL), 0.05),
        'w_in': nrm((L, D_MODEL, D_IN), D_MODEL ** -0.5),
        'g_q': 1.0 + nrm((L, HEAD_DIM), 0.05),
        'g_k': 1.0 + nrm((L, HEAD_DIM), 0.05),
        'rel_bias': nrm((L, N_HEADS, N_REL), 0.1),
        'conv_w': nrm((L, CONV_WIDTH, D_CONV), CONV_WIDTH ** -0.5),
        'conv_b': nrm((L, D_CONV), 0.02),
        'conv_ln_g': 1.0 + nrm((L, D_CONV), 0.05),
        'conv_ln_b': nrm((L, D_CONV), 0.02),
        'beta_attn': 1.0 + nrm((L, D_ATTN), 0.05),
        'beta_conv': 1.0 + nrm((L, D_CONV), 0.05),
        'w_out': nrm((L, D_MODEL, D_MODEL), D_MODEL ** -0.5),
        'g_ffn': 1.0 + nrm((L, D_MODEL), 0.05),
        'w_router': nrm((L, D_MODEL, N_EXPERTS), D_MODEL ** -0.5),
        'router_bias': nrm((L, N_EXPERTS), 0.01),
        'w_gate': nrm((L, N_EXPERTS, D_MODEL, D_EXPERT), D_MODEL ** -0.5),
        'w_up': nrm((L, N_EXPERTS, D_MODEL, D_EXPERT), D_MODEL ** -0.5),
        'w_down': nrm((L, N_EXPERTS, D_EXPERT, D_MODEL), D_EXPERT ** -0.5),
        'ws_gate': nrm((L, D_MODEL, D_SHARED), D_MODEL ** -0.5),
        'ws_up': nrm((L, D_MODEL, D_SHARED), D_MODEL ** -0.5),
        'ws_down': nrm((L, D_SHARED, D_MODEL), D_SHARED ** -0.5),
    }


def reference(x_prompt, x_sample, c_prompt, c_sample, cache_k, cache_v, cache_conv,
              w_ada, b_ada, g_mix, w_in, g_q, g_k, rel_bias, conv_w, conv_b, conv_ln_g, conv_ln_b,
              beta_attn, beta_conv, w_out, g_ffn, w_router, router_bias, w_gate, w_up, w_down,
              ws_gate, ws_up, ws_down):
    h_p, h_s = x_prompt, x_sample
    nk_p, nv_p, nc_p, nk_s, nv_s, nc_s = [], [], [], [], [], []
    for l in range(DEPTH):
        mix_w = dict(g_mix=g_mix[l], w_in=w_in[l], g_q=g_q[l], g_k=g_k[l], rel_bias=rel_bias[l],
                     conv_w=conv_w[l], conv_b=conv_b[l], conv_ln_g=conv_ln_g[l], conv_ln_b=conv_ln_b[l],
                     beta_attn=beta_attn[l], beta_conv=beta_conv[l], w_out=w_out[l])
        ffn_w = dict(g_ffn=g_ffn[l], w_router=w_router[l], router_bias=router_bias[l],
                     w_gate=w_gate[l], w_up=w_up[l], w_down=w_down[l],
                     ws_gate=ws_gate[l], ws_up=ws_up[l], ws_down=ws_down[l])
        sp1, sc1, gt1, sp2, sc2, gt2 = adaln(c_prompt, w_ada[l], b_ada[l])
        ss1, ss1c, gs1, ss2, ss2c, gs2 = adaln(c_sample, w_ada[l], b_ada[l])
        h_p, kp_, vp_, cp_ = mixer_sublayer(h_p, sp1, sc1, gt1, None, None, None, **mix_w)
        h_p = ffn_sublayer(h_p, sp2, sc2, gt2, **ffn_w)
        h_s, ks_, vs_, cs_ = mixer_sublayer(h_s, ss1, ss1c, gs1, cache_k[l], cache_v[l], cache_conv[l], **mix_w)
        h_s = ffn_sublayer(h_s, ss2, ss2c, gs2, **ffn_w)
        nk_p.append(kp_); nv_p.append(vp_); nc_p.append(cp_)
        nk_s.append(ks_); nv_s.append(vs_); nc_s.append(cs_)
    new_k_prompt = jnp.stack(nk_p)
    new_v_prompt = jnp.stack(nv_p)
    new_conv_prompt = jnp.stack(nc_p)
    new_k_sample = jnp.stack(nk_s)
    new_v_sample = jnp.stack(nv_s)
    new_conv_sample = jnp.stack(nc_s)
    return (h_p, h_s, new_k_prompt, new_v_prompt, new_conv_prompt, new_k_sample, new_v_sample, new_conv_sample)
```

```python
import functools

import jax
import jax.numpy as jnp
from jax import lax
from jax.experimental import pallas as pl
from jax.experimental.pallas import tpu as pltpu

F32 = jnp.float32
BF16 = jnp.bfloat16
U32 = jnp.uint32
I32 = jnp.int32

CHUNK = 64
N_LEFT_CHUNKS = 8
LEFT_REACH = N_LEFT_CHUNKS * CHUNK
MAX_REL = 128
N_GROUPS = 8
TOPK_GROUPS = 4
TOP_K = 8
ROUTED_SCALE = 2.5
EPS = 1e-6
NEG_INF = -1e30

CONV_PAD_ROWS = 32
V7X_VMEM_LIMIT = 56 * 1024 * 1024


def _cparams(n_axes):
    return pltpu.CompilerParams(dimension_semantics=("arbitrary",) * n_axes,
                                vmem_limit_bytes=V7X_VMEM_LIMIT)


def _tile(n, pref):
    t = min(n, pref)
    while n % t:
        t //= 2
    return t


def _rows_block(batch, seq, rows):
    st = min(seq, rows)
    bt = rows // st
    assert bt * st == rows and batch % bt == 0 and seq % st == 0
    return bt, st


def _row_maps(bt, spb):
    if bt == 1:
        return (lambda i: (i // spb, i % spb)), (lambda i: i // spb)
    return (lambda i: (i, 0)), (lambda i: i)


def _pack_bf16_pair(a, b):
    au = lax.bitcast_convert_type(a.astype(BF16).astype(F32), U32)
    bu = lax.bitcast_convert_type(b.astype(BF16).astype(F32), U32)
    return au | (bu >> 16)


def _unpack_bf16_pair(p):
    a = lax.bitcast_convert_type(p & jnp.uint32(0xFFFF0000), F32)
    b = lax.bitcast_convert_type(p << 16, F32)
    return a, b


def _silu(x):
    return x * jax.nn.sigmoid(x)


def _ada_kernel(c_ref, w_ref, b_ref, o_ref):
    a = _silu(c_ref[...]).astype(BF16)
    o_ref[...] = jnp.dot(a, w_ref[...].astype(BF16), preferred_element_type=F32) + b_ref[...]


def _ada(c, w, b):
    m, d = c.shape
    n = w.shape[1]
    tn = _tile(n, 512)
    return pl.pallas_call(
        _ada_kernel,
        grid=(n // tn,),
        in_specs=[pl.BlockSpec((m, d), lambda j: (0, 0)),
                  pl.BlockSpec((d, tn), lambda j: (0, j)),
                  pl.BlockSpec((1, tn), lambda j: (0, j))],
        out_specs=pl.BlockSpec((m, tn), lambda j: (0, j)),
        out_shape=jax.ShapeDtypeStruct((m, n), F32),
        compiler_params=_cparams(1),
        name="ada",
    )(c, w, b.reshape(1, n))


def _modnorm(x, g, shift, scale):
    ms = jnp.mean(x * x, axis=-1, keepdims=True)
    y = x * lax.rsqrt(ms + EPS) * g
    return y * (1.0 + scale) + shift


def _inproj_kernel(x_ref, sh_ref, sc_ref, g_ref, w_ref, gh_ref, o_ref, hn_ref, *, n_norm_tiles, head_dim):
    j = pl.program_id(1)
    tm, d = hn_ref.shape
    tn = o_ref.shape[1]

    @pl.when(j == 0)
    def _():
        h = _modnorm(x_ref[...], g_ref[...], sh_ref[...], sc_ref[...])
        hn_ref[...] = h.reshape(tm, d).astype(BF16)

    acc = jnp.dot(hn_ref[...], w_ref[...], preferred_element_type=F32)

    @pl.when(j < n_norm_tiles)
    def _():
        for hh in range(tn // head_dim):
            sl = slice(hh * head_dim, (hh + 1) * head_dim)
            a = acc[:, sl]
            ms = jnp.mean(a * a, axis=-1, keepdims=True)
            o_ref[:, sl] = (a * lax.rsqrt(ms + EPS) * gh_ref[:, sl]).astype(BF16)

    @pl.when(j >= n_norm_tiles)
    def _():
        o_ref[...] = acc.astype(BF16)


def _inproj(x, shift, scale, g, w_bf, gh, head_dim):
    b, s, d = x.shape
    n = w_bf.shape[1]
    rows = b * s
    tm = _tile(rows, 512)
    tn = _tile(n, 1024)
    bt, st = _rows_block(b, s, tm)
    n_norm = gh.shape[1] // tn
    assert n_norm * tn == gh.shape[1] and tn % head_dim == 0
    spb = s // st
    rmap, bmap = _row_maps(bt, spb)
    xmap = lambda i, j: (*rmap(i), 0)
    mmap = lambda i, j: (bmap(i), 0, 0)
    return pl.pallas_call(
        functools.partial(_inproj_kernel, n_norm_tiles=n_norm, head_dim=head_dim),
        grid=(rows // tm, n // tn),
        in_specs=[pl.BlockSpec((bt, st, d), xmap),
                  pl.BlockSpec((bt, 1, d), mmap),
                  pl.BlockSpec((bt, 1, d), mmap),
                  pl.BlockSpec((1, 1, d), lambda i, j: (0, 0, 0)),
                  pl.BlockSpec((d, tn), lambda i, j: (0, j)),
                  pl.BlockSpec((1, tn), lambda i, j: (0, jnp.minimum(j, n_norm - 1)))],
        out_specs=pl.BlockSpec((tm, tn), lambda i, j: (i, j)),
        out_shape=jax.ShapeDtypeStruct((rows, n), BF16),
        scratch_shapes=[pltpu.VMEM((tm, d), BF16)],
        compiler_params=_cparams(2),
        name="inproj",
    )(x, shift, scale, g.reshape(1, 1, d), w_bf, gh)


def _softmax_pv(s_list, v_list):
    m = functools.reduce(jnp.maximum, [jnp.max(s, axis=-1, keepdims=True) for s in s_list])
    p_list = [jnp.exp(s - m) for s in s_list]
    l = functools.reduce(jnp.add, [jnp.sum(p, axis=-1, keepdims=True) for p in p_list])
    o = functools.reduce(jnp.add, [jnp.dot(p.astype(BF16), v, preferred_element_type=F32)
                                   for p, v in zip(p_list, v_list)])
    return o / l


def _nt_dot(a, b):
    return lax.dot_general(a, b, (((1,), (1,)), ((), ())), preferred_element_type=F32)


def _attn_prompt_kernel(q_ref, k0_ref, k1_ref, k2_ref, v0_ref, v1_ref, v2_ref, t_ref, o_ref, *, scale):
    i = pl.program_id(2)
    qb = q_ref.shape[1]
    q = q_ref[0]
    k = jnp.concatenate([k0_ref[0], k1_ref[0], k2_ref[0]], axis=0)
    v = jnp.concatenate([v0_ref[0], v1_ref[0], v2_ref[0]], axis=0)
    s = _nt_dot(q, k) * scale + t_ref[0]
    col = lax.broadcasted_iota(I32, s.shape, 1)
    s = jnp.where(col < (2 - i) * qb, NEG_INF, s)
    o_ref[0] = _softmax_pv([s], [v]).astype(BF16)


def _attn_prompt(z3, table, n_heads, head_dim, d_attn):
    b, s, _ = z3.shape
    qb = LEFT_REACH // 2
    assert s % qb == 0 and table.shape == (n_heads, qb, 3 * qb)
    hk, hv = d_attn // head_dim, 2 * d_attn // head_dim

    def kv_spec(off, back):
        return pl.BlockSpec((1, qb, head_dim), lambda bb, h, i: (bb, jnp.maximum(i - back, 0), off + h))

    scale = head_dim ** -0.5
    return pl.pallas_call(
        functools.partial(_attn_prompt_kernel, scale=scale),
        grid=(b, n_heads, s // qb),
        in_specs=[pl.BlockSpec((1, qb, head_dim), lambda bb, h, i: (bb, i, h)),
                  kv_spec(hk, 2), kv_spec(hk, 1), kv_spec(hk, 0),
                  kv_spec(hv, 2), kv_spec(hv, 1), kv_spec(hv, 0),
                  pl.BlockSpec((1, qb, 3 * qb), lambda bb, h, i: (h, 0, 0))],
        out_specs=pl.BlockSpec((1, qb, head_dim), lambda bb, h, i: (bb, i, h)),
        out_shape=jax.ShapeDtypeStruct((b, s, d_attn), BF16),
        compiler_params=_cparams(3),
        name="attn_prompt",
    )(z3, z3, z3, z3, z3, z3, z3, table)


def _attn_sample_kernel(q_ref, kn_ref, vn_ref, ck_ref, cv_ref, tc_ref, tn_ref, o_ref, *, scale):
    q = q_ref[0]
    ck = ck_ref[0, 0].astype(BF16)
    cv = cv_ref[0, 0].astype(BF16)
    s_c = _nt_dot(q, ck) * scale + tc_ref[0]
    s_n = _nt_dot(q, kn_ref[0]) * scale + tn_ref[0]
    o_ref[0] = _softmax_pv([s_c, s_n], [cv, vn_ref[0]]).astype(BF16)


def _attn_sample(z3, cache_k, cache_v, tab_c, tab_n, n_heads, head_dim, d_attn):
    b, t, _ = z3.shape
    w = cache_k.shape[2]
    hk, hv = d_attn // head_dim, 2 * d_attn // head_dim
    scale = head_dim ** -0.5
    return pl.pallas_call(
        functools.partial(_attn_sample_kernel, scale=scale),
        grid=(b, n_heads),
        in_specs=[pl.BlockSpec((1, t, head_dim), lambda bb, h: (bb, 0, h)),
                  pl.BlockSpec((1, t, head_dim), lambda bb, h: (bb, 0, hk + h)),
                  pl.BlockSpec((1, t, head_dim), lambda bb, h: (bb, 0, hv + h)),
                  pl.BlockSpec((1, 1, w, head_dim), lambda bb, h: (bb, h, 0, 0)),
                  pl.BlockSpec((1, 1, w, head_dim), lambda bb, h: (bb, h, 0, 0)),
                  pl.BlockSpec((1, t, w), lambda bb, h: (h, 0, 0)),
                  pl.BlockSpec((1, t, t), lambda bb, h: (h, 0, 0))],
        out_specs=pl.BlockSpec((1, t, head_dim), lambda bb, h: (bb, 0, h)),
        out_shape=jax.ShapeDtypeStruct((b, t, d_attn), BF16),
        compiler_params=_cparams(2),
        name="attn_sample",
    )(z3, z3, z3, cache_k, cache_v, tab_c, tab_n)


def _bias_tables(rel_bias, qb, t_new, past_win):
    def lookup(delta):
        return rel_bias[:, jnp.clip(delta, -MAX_REL, MAX_REL) + MAX_REL]

    r = jnp.arange(qb)[:, None]
    j = jnp.arange(3 * qb)[None, :]
    dchunk = j // CHUNK - r // CHUNK
    ok = (dchunk >= 0) & (dchunk <= N_LEFT_CHUNKS)
    tab_p = jnp.where(ok[None], lookup(j - 2 * qb - r), NEG_INF)
    rs = jnp.arange(t_new)[:, None]
    tab_c = lookup(jnp.arange(past_win)[None, :] - past_win - rs)
    tab_n = lookup(jnp.arange(t_new)[None, :] - rs)
    return tab_p.astype(F32), tab_c.astype(F32), tab_n.astype(F32)


def _conv_kernel(a_ref, g_ref, pre_ref, cw_ref, cb_ref, lg_ref, lb_ref, beta_ref, o_ref, tail_ref, buf_ref,
                 *, width, rc):
    t = pl.program_id(1)
    ts = a_ref.shape[1]
    pad = CONV_PAD_ROWS

    @pl.when(t == 0)
    def _():
        buf_ref[0:pad, :] = pre_ref[0]

    @pl.when(t > 0)
    def _():
        buf_ref[0:pad, :] = buf_ref[ts:ts + pad, :]

    buf_ref[pad:pad + ts, :] = a_ref[0].astype(F32) * jax.nn.sigmoid(g_ref[0].astype(F32))
    tail_ref[0] = buf_ref[ts:ts + pad, :]
    first = pad - (width - 1)

    def body(ci, carry):
        base = pl.multiple_of(ci * rc, rc)
        win = buf_ref[pl.ds(base, rc + pad), :]
        acc = win[first:first + rc, :] * cw_ref[0:1, :]
        for w in range(1, width):
            acc = acc + win[first + w:first + w + rc, :] * cw_ref[w:w + 1, :]
        y = acc + cb_ref[...]
        mu = jnp.mean(y, axis=-1, keepdims=True)
        yc = y - mu
        var = jnp.mean(yc * yc, axis=-1, keepdims=True)
        s = _silu(yc * lax.rsqrt(var + EPS) * lg_ref[...] + lb_ref[...])
        ms = jnp.mean(s * s, axis=-1, keepdims=True)
        o_ref[0, pl.ds(base, rc), :] = (s * lax.rsqrt(ms + EPS) * beta_ref[...]).astype(BF16)
        return carry

    lax.fori_loop(0, ts // rc, body, 0)


def _conv(z3, prefix, conv_w, conv_b, ln_g, ln_b, beta, d_attn, d_conv):
    b, t, _ = z3.shape
    width = conv_w.shape[0]
    ts = _tile(t, 256)
    rc = 16
    assert 3 * d_attn % d_conv == 0 and ts % rc == 0 and ts >= CONV_PAD_ROWS
    ca = 3 * d_attn // d_conv
    vec = lambda x: x.reshape(1, d_conv)
    cst = lambda shape: pl.BlockSpec(shape, lambda bb, tt: (0,) * len(shape))
    return pl.pallas_call(
        functools.partial(_conv_kernel, width=width, rc=rc),
        grid=(b, t // ts),
        in_specs=[pl.BlockSpec((1, ts, d_conv), lambda bb, tt: (bb, tt, ca)),
                  pl.BlockSpec((1, ts, d_conv), lambda bb, tt: (bb, tt, ca + 1)),
                  pl.BlockSpec((1, CONV_PAD_ROWS, d_conv), lambda bb, tt: (bb, 0, 0)),
                  cst((width, d_conv)), cst((1, d_conv)), cst((1, d_conv)), cst((1, d_conv)), cst((1, d_conv))],
        out_specs=[pl.BlockSpec((1, ts, d_conv), lambda bb, tt: (bb, tt, 0)),
                   pl.BlockSpec((1, CONV_PAD_ROWS, d_conv), lambda bb, tt: (bb, 0, 0))],
        out_shape=[jax.ShapeDtypeStruct((b, t, d_conv), BF16),
                   jax.ShapeDtypeStruct((b, CONV_PAD_ROWS, d_conv), F32)],
        scratch_shapes=[pltpu.VMEM((CONV_PAD_ROWS + ts, d_conv), F32)],
        compiler_params=_cparams(2),
        name="conv",
    )(z3, z3, prefix, conv_w, vec(conv_b), vec(ln_g), vec(ln_b), vec(beta))


def _outproj_kernel(oa_ref, oc_ref, ba_ref, w_ref, x_ref, gt_ref, o_ref, mx_ref):
    j = pl.program_id(1)
    da = oa_ref.shape[1]

    @pl.when(j == 0)
    def _():
        oa = oa_ref[...].astype(F32)
        ms = jnp.mean(oa * oa, axis=-1, keepdims=True)
        mx_ref[:, :da] = (oa * lax.rsqrt(ms + EPS) * ba_ref[...]).astype(BF16)
        mx_ref[:, da:] = oc_ref[...]

    acc = jnp.dot(mx_ref[...], w_ref[...], preferred_element_type=F32)
    o_ref[...] = x_ref[...] + gt_ref[...] * acc.reshape(x_ref.shape)


def _outproj(o_attn, o_conv, beta_attn, w_bf, x, gate):
    b, s, d = x.shape
    rows = b * s
    da, dc = o_attn.shape[1], o_conv.shape[1]
    tm = _tile(rows, 512)
    tn = _tile(d, 1024)
    bt, st = _rows_block(b, s, tm)
    spb = s // st
    rmap, bmap = _row_maps(bt, spb)
    xmap = lambda i, j: (*rmap(i), j)
    gmap = lambda i, j: (bmap(i), 0, j)
    return pl.pallas_call(
        _outproj_kernel,
        grid=(rows // tm, d // tn),
        in_specs=[pl.BlockSpec((tm, da), lambda i, j: (i, 0)),
                  pl.BlockSpec((tm, dc), lambda i, j: (i, 0)),
                  pl.BlockSpec((1, da), lambda i, j: (0, 0)),
                  pl.BlockSpec((da + dc, tn), lambda i, j: (0, j)),
                  pl.BlockSpec((bt, st, tn), xmap),
                  pl.BlockSpec((bt, 1, tn), gmap)],
        out_specs=pl.BlockSpec((bt, st, tn), xmap),
        out_shape=jax.ShapeDtypeStruct((b, s, d), F32),
        scratch_shapes=[pltpu.VMEM((tm, da + dc), BF16)],
        compiler_params=_cparams(2),
        name="outproj",
    )(o_attn, o_conv, beta_attn.reshape(1, da), w_bf, x, gate)


def _router_kernel(h_ref, sh_ref, sc_ref, g_ref, wh_ref, wl_ref, rb_ref, hp_ref, ei_ref, gk_ref):
    tm, dh = hp_ref.shape
    n_exp = wh_ref.shape[0]
    gsz = n_exp // N_GROUPS
    h = _modnorm(h_ref[...], g_ref[...], sh_ref[...], sc_ref[...]).reshape(tm, 2 * dh)
    hp_ref[...] = _pack_bf16_pair(h[:, :dh], h[:, dh:])
    hi = h.astype(BF16)
    lo = (h - hi.astype(F32)).astype(BF16)
    logits = _nt_dot(wh_ref[...], hi) + (_nt_dot(wh_ref[...], lo) + _nt_dot(wl_ref[...], hi))
    scores = jax.nn.sigmoid(logits)
    choice = scores + rb_ref[...]

    sub = lax.broadcasted_iota(I32, (gsz, tm), 0).astype(F32)
    blocks, gs_rows = [], []
    for g in range(N_GROUPS):
        blk = choice[g * gsz:(g + 1) * gsz, :]
        m1 = jnp.max(blk, axis=0, keepdims=True)
        first = jnp.min(jnp.where(blk == m1, sub, float(gsz)), axis=0, keepdims=True)
        m2 = jnp.max(jnp.where(sub == first, -jnp.inf, blk), axis=0, keepdims=True)
        blocks.append(blk)
        gs_rows.append(m1 + m2)
    gs = jnp.concatenate(gs_rows, axis=0)

    def rank_desc(x):
        rows = lax.broadcasted_iota(I32, x.shape, 0)
        rank = jnp.zeros(x.shape, I32)
        for r2 in range(x.shape[0]):
            other = x[r2:r2 + 1, :]
            ahead = jnp.where(other > x, 1, jnp.where((other == x) & (rows > r2), 1, 0))
            rank = rank + ahead
        return rank, rows

    grank, _ = rank_desc(gs)
    masked = jnp.concatenate(
        [jnp.where(grank[g:g + 1, :] < TOPK_GROUPS, blocks[g], NEG_INF) for g in range(N_GROUPS)], axis=0)
    erank, erow = rank_desc(masked)
    w = jnp.where(erank < TOP_K, scores, 0.0)
    gates = w / jnp.sum(w, axis=0, keepdims=True) * ROUTED_SCALE
    erow_f = erow.astype(F32)
    ei_rows, gk_rows = [], []
    for r in range(TOP_K):
        hit = erank == r
        ei_rows.append(jnp.sum(jnp.where(hit, erow_f, 0.0), axis=0, keepdims=True))
        gk_rows.append(jnp.sum(jnp.where(hit, gates, 0.0), axis=0, keepdims=True))
    ei_ref[...] = jnp.concatenate(ei_rows, axis=0).astype(I32)
    gk_ref[...] = jnp.concatenate(gk_rows, axis=0)


def _router(h1, shift, scale, g, wr_hi, wr_lo, rbias):
    b, s, d = h1.shape
    rows = b * s
    n_exp = wr_hi.shape[0]
    tm = _tile(rows, 512)
    bt, st = _rows_block(b, s, tm)
    spb = s // st
    rmap, bmap = _row_maps(bt, spb)
    xmap = lambda i: (*rmap(i), 0)
    mmap = lambda i: (bmap(i), 0, 0)
    return pl.pallas_call(
        _router_kernel,
        grid=(rows // tm,),
        in_specs=[pl.BlockSpec((bt, st, d), xmap),
                  pl.BlockSpec((bt, 1, d), mmap),
                  pl.BlockSpec((bt, 1, d), mmap),
                  pl.BlockSpec((1, 1, d), lambda i: (0, 0, 0)),
                  pl.BlockSpec((n_exp, d), lambda i: (0, 0)),
                  pl.BlockSpec((n_exp, d), lambda i: (0, 0)),
                  pl.BlockSpec((n_exp, 1), lambda i: (0, 0))],
        out_specs=[pl.BlockSpec((tm, d // 2), lambda i: (i, 0)),
                   pl.BlockSpec((TOP_K, tm), lambda i: (0, i)),
                   pl.BlockSpec((TOP_K, tm), lambda i: (0, i))],
        out_shape=[jax.ShapeDtypeStruct((rows, d // 2), U32),
                   jax.ShapeDtypeStruct((TOP_K, rows), I32),
                   jax.ShapeDtypeStruct((TOP_K, rows), F32)],
        compiler_params=_cparams(1),
        name="router",
    )(h1, shift, scale, g.reshape(1, 1, d), wr_hi, wr_lo, rbias.reshape(n_exp, 1))


def _gather_kernel(idx_ref, src_ref, out_ref, sem):
    c = pl.program_id(0)
    chunk = idx_ref.shape[2]
    base = c * chunk

    def row_copy(src_row, dst_row):
        return pltpu.make_async_copy(src_ref.at[pl.ds(src_row, 1)], out_ref.at[pl.ds(dst_row, 1)], sem)

    def start(i, carry):
        row_copy(idx_ref[0, 0, i], base + i).start()
        return carry

    def wait(i, carry):
        row_copy(0, 0).wait()
        return carry

    lax.fori_loop(0, chunk, start, 0)
    lax.fori_loop(0, chunk, wait, 0)


def _gather_rows(src, idx):
    n = idx.shape[0]
    chunk = _tile(n, 512)
    return pl.pallas_call(
        _gather_kernel,
        grid=(n // chunk,),
        in_specs=[pl.BlockSpec((1, 1, chunk), lambda c: (c, 0, 0), memory_space=pltpu.SMEM),
                  pl.BlockSpec(memory_space=pl.ANY)],
        out_specs=pl.BlockSpec(memory_space=pl.ANY),
        scratch_shapes=[pltpu.SemaphoreType.DMA(())],
        out_shape=jax.ShapeDtypeStruct((n, src.shape[1]), src.dtype),
        compiler_params=_cparams(1),
        name="gather_rows",
    )(idx.reshape(n // chunk, 1, chunk), src)


def _expert_changed(te_ref, t):
    return (t == 0) | (te_ref[t] != te_ref[jnp.maximum(t - 1, 0)])


def _gateup_kernel(te_ref, nt_ref, x_ref, wg_ref, wu_ref, h_ref, wgb_ref, wub_ref):
    t = pl.program_id(1)
    dh = x_ref.shape[1]

    @pl.when(t < nt_ref[0])
    def _():
        @pl.when(_expert_changed(te_ref, t))
        def _():
            wgb_ref[...] = wg_ref[0].astype(BF16)
            wub_ref[...] = wu_ref[0].astype(BF16)

        xa, xb = _unpack_bf16_pair(x_ref[...])
        xa, xb = xa.astype(BF16), xb.astype(BF16)
        g = (jnp.dot(xa, wgb_ref[:dh, :], preferred_element_type=F32)
             + jnp.dot(xb, wgb_ref[dh:, :], preferred_element_type=F32))
        u = (jnp.dot(xa, wub_ref[:dh, :], preferred_element_type=F32)
             + jnp.dot(xb, wub_ref[dh:, :], preferred_element_type=F32))
        h_ref[...] = (_silu(g) * u).astype(BF16)

    @pl.when(t >= nt_ref[0])
    def _():
        h_ref[...] = jnp.zeros(h_ref.shape, h_ref.dtype)


def _down_kernel(te_ref, nt_ref, h_ref, wd_ref, y_ref, wdb_ref):
    t = pl.program_id(0)
    dh = y_ref.shape[1]

    @pl.when(t < nt_ref[0])
    def _():
        @pl.when(_expert_changed(te_ref, t))
        def _():
            wdb_ref[...] = wd_ref[0].astype(BF16)

        y = jnp.dot(h_ref[...], wdb_ref[...], preferred_element_type=F32)
        y_ref[...] = _pack_bf16_pair(y[:, :dh], y[:, dh:])

    @pl.when(t >= nt_ref[0])
    def _():
        y_ref[...] = jnp.zeros(y_ref.shape, y_ref.dtype)


def _expert_ffn(xs, tile_expert, n_tiles, w_gate, w_up, w_down, tm):
    s_rows, dh = xs.shape
    n_exp, d, f = w_gate.shape
    nt = s_rows // tm
    tf = _tile(f, 512)
    last = lambda t, ntr: jnp.minimum(t, ntr[0] - 1)
    hmid = pl.pallas_call(
        _gateup_kernel,
        grid_spec=pltpu.PrefetchScalarGridSpec(
            num_scalar_prefetch=2,
            grid=(f // tf, nt),
            in_specs=[pl.BlockSpec((tm, dh), lambda j, t, te, ntr: (last(t, ntr), 0)),
                      pl.BlockSpec((1, d, tf), lambda j, t, te, ntr: (te[t], 0, j)),
                      pl.BlockSpec((1, d, tf), lambda j, t, te, ntr: (te[t], 0, j))],
            out_specs=pl.BlockSpec((tm, tf), lambda j, t, te, ntr: (t, j)),
            scratch_shapes=[pltpu.VMEM((d, tf), BF16), pltpu.VMEM((d, tf), BF16)]),
        out_shape=jax.ShapeDtypeStruct((s_rows, f), BF16),
        compiler_params=_cparams(2),
        name="expert_gateup",
    )(tile_expert, n_tiles, xs, w_gate, w_up)
    return pl.pallas_call(
        _down_kernel,
        grid_spec=pltpu.PrefetchScalarGridSpec(
            num_scalar_prefetch=2,
            grid=(nt,),
            in_specs=[pl.BlockSpec((tm, f), lambda t, te, ntr: (last(t, ntr), 0)),
                      pl.BlockSpec((1, f, d), lambda t, te, ntr: (te[t], 0, 0))],
            out_specs=pl.BlockSpec((tm, dh), lambda t, te, ntr: (t, 0)),
            scratch_shapes=[pltpu.VMEM((f, d), BF16)]),
        out_shape=jax.ShapeDtypeStruct((s_rows, dh), U32),
        compiler_params=_cparams(1),
        name="expert_down",
    )(tile_expert, n_tiles, hmid, w_down)


def _shared_kernel(x_ref, wg_ref, wu_ref, wd_ref, o_ref, xa_ref, xb_ref, acc_ref):
    j = pl.program_id(1)
    dh = x_ref.shape[1]

    @pl.when(j == 0)
    def _():
        xa, xb = _unpack_bf16_pair(x_ref[...])
        xa_ref[...] = xa.astype(BF16)
        xb_ref[...] = xb.astype(BF16)

    xa, xb = xa_ref[...], xb_ref[...]
    g = (jnp.dot(xa, wg_ref[:dh, :], preferred_element_type=F32)
         + jnp.dot(xb, wg_ref[dh:, :], preferred_element_type=F32))
    u = (jnp.dot(xa, wu_ref[:dh, :], preferred_element_type=F32)
         + jnp.dot(xb, wu_ref[dh:, :], preferred_element_type=F32))
    part = jnp.dot((_silu(g) * u).astype(BF16), wd_ref[...], preferred_element_type=F32)

    @pl.when(j == 0)
    def _():
        acc_ref[...] = part

    @pl.when(j > 0)
    def _():
        acc_ref[...] += part

    @pl.when(j == pl.num_programs(1) - 1)
    def _():
        o_ref[...] = acc_ref[...].astype(BF16)


def _shared_ffn(xp, wg_bf, wu_bf, wd_bf):
    rows, dh = xp.shape
    d, f = wg_bf.shape
    tm = _tile(rows, 512)
    tf = _tile(f, 256)
    return pl.pallas_call(
        _shared_kernel,
        grid=(rows // tm, f // tf),
        in_specs=[pl.BlockSpec((tm, dh), lambda i, j: (i, 0)),
                  pl.BlockSpec((d, tf), lambda i, j: (0, j)),
                  pl.BlockSpec((d, tf), lambda i, j: (0, j)),
                  pl.BlockSpec((tf, d), lambda i, j: (j, 0))],
        out_specs=pl.BlockSpec((tm, d), lambda i, j: (i, 0)),
        out_shape=jax.ShapeDtypeStruct((rows, d), BF16),
        scratch_shapes=[pltpu.VMEM((tm, dh), BF16), pltpu.VMEM((tm, dh), BF16), pltpu.VMEM((tm, d), F32)],
        compiler_params=_cparams(2),
        name="shared_ffn",
    )(xp, wg_bf, wu_bf, wd_bf)


def _combine_kernel(h_ref, gt_ref, ys_ref, yu_ref, gk_ref, o_ref):
    tc, d = ys_ref.shape
    dh = d // 2
    ys = ys_ref[...].astype(F32)
    acc_a, acc_b = ys[:, :dh], ys[:, dh:]
    for r in range(yu_ref.shape[0]):
        ya, yb = _unpack_bf16_pair(yu_ref[r])
        gr = gk_ref[:, r:r + 1]
        acc_a = acc_a + gr * ya
        acc_b = acc_b + gr * yb
    shp = (h_ref.shape[0], h_ref.shape[1], dh)
    o_ref[:, :, :dh] = h_ref[:, :, :dh] + gt_ref[:, :, :dh] * acc_a.reshape(shp)
    o_ref[:, :, dh:] = h_ref[:, :, dh:] + gt_ref[:, :, dh:] * acc_b.reshape(shp)


def _combine(h1, gate, ysh, yu, gk, row0):
    b, s, d = h1.shape
    rows = b * s
    tc = _tile(rows, 128)
    bt, st = _rows_block(b, s, tc)
    spb = s // st
    assert row0 % tc == 0
    t0 = row0 // tc
    rmap, bmap = _row_maps(bt, spb)
    xmap = lambda i: (*rmap(i), 0)
    gmap = lambda i: (bmap(i), 0, 0)
    k = yu.shape[0]
    return pl.pallas_call(
        _combine_kernel,
        grid=(rows // tc,),
        in_specs=[pl.BlockSpec((bt, st, d), xmap),
                  pl.BlockSpec((bt, 1, d), gmap),
                  pl.BlockSpec((tc, d), lambda i: (t0 + i, 0)),
                  pl.BlockSpec((k, tc, d // 2), lambda i: (0, t0 + i, 0)),
                  pl.BlockSpec((tc, k), lambda i: (t0 + i, 0))],
        out_specs=pl.BlockSpec((bt, st, d), xmap),
        out_shape=jax.ShapeDtypeStruct((b, s, d), F32),
        compiler_params=_cparams(1),
        name="combine",
    )(h1, gate, ysh, yu, gk)


def _routing_tables(eidx, n_exp, tm):
    k, n = eidx.shape
    pairs = k * n
    e_flat = eidx.reshape(pairs)
    order = jnp.argsort(e_flat, stable=True).astype(I32)
    counts = jnp.sum((e_flat[None, :] == jnp.arange(n_exp, dtype=I32)[:, None]).astype(I32), axis=1)
    start = jnp.cumsum(counts) - counts
    tiles = (counts + tm - 1) // tm
    tile_end = jnp.cumsum(tiles)
    off = (tile_end - tiles) * tm
    n_tiles = tile_end[-1:].astype(I32)
    nt_max = pairs // tm + n_exp
    tile_expert = jnp.minimum(jnp.searchsorted(tile_end, jnp.arange(nt_max, dtype=I32), side="right"),
                              n_exp - 1).astype(I32)
    tile_expert = jnp.where(jnp.arange(nt_max) < n_tiles[0], tile_expert, tile_expert[jnp.maximum(n_tiles[0] - 1, 0)])
    slot = jnp.arange(nt_max * tm, dtype=I32)
    se = tile_expert[slot // tm]
    j = slot - off[se]
    valid = (j < counts[se]) & (slot < n_tiles[0] * tm)
    pair_of_slot = order[jnp.clip(start[se] + j, 0, pairs - 1)]
    token_of_slot = jnp.where(valid, pair_of_slot % n, 0).astype(I32)
    inv = jnp.zeros((pairs,), I32).at[order].set(jnp.arange(pairs, dtype=I32))
    slot_of_pair = (inv - start[e_flat] + off[e_flat]).astype(I32)
    return token_of_slot, slot_of_pair, tile_expert, n_tiles


def kernel(x_prompt, x_sample, c_prompt, c_sample, cache_k, cache_v, cache_conv, w_ada, b_ada, g_mix, w_in, g_q, g_k, rel_bias, conv_w, conv_b, conv_ln_g, conv_ln_b, beta_attn, beta_conv, w_out, g_ffn, w_router, router_bias, w_gate, w_up, w_down, ws_gate, ws_up, ws_down):
    depth = w_in.shape[0]
    assert depth == 1
    bp, sp, d = x_prompt.shape
    bs, ss, _ = x_sample.shape
    head_dim = g_q.shape[1]
    n_heads = rel_bias.shape[1]
    d_attn = n_heads * head_dim
    d_conv = conv_w.shape[2]
    width = conv_w.shape[1]
    n_exp = w_router.shape[2]
    past_win = cache_k.shape[3]
    l = 0

    n_c = bp + bs
    c_all = jnp.concatenate([c_prompt, c_sample], axis=0)
    c_all = jnp.pad(c_all, ((0, -n_c % 16), (0, 0)))
    mod = _ada(c_all, w_ada[l], b_ada[l])
    mod_p = mod[:bp].reshape(bp, 1, 6 * d)
    mod_s = mod[bp:n_c].reshape(bs, 1, 6 * d)
    part = lambda m, i: m[:, :, i * d:(i + 1) * d]

    w_in_bf = w_in[l].astype(BF16)
    w_out_bf = w_out[l].astype(BF16)
    gh = jnp.concatenate([jnp.tile(g_q[l], n_heads), jnp.tile(g_k[l], n_heads)]).reshape(1, 2 * d_attn)
    tab_p, tab_c, tab_n = _bias_tables(rel_bias[l], LEFT_REACH // 2, ss, past_win)
    wr_t = w_router[l].T
    wr_hi = wr_t.astype(BF16)
    wr_lo = (wr_t - wr_hi.astype(F32)).astype(BF16)

    def mixer(x, m, attn_fn, prefix):
        b, s, _ = x.shape
        z = _inproj(x, part(m, 0), part(m, 1), g_mix[l], w_in_bf, gh, head_dim)
        z3 = z.reshape(b, s, z.shape[1])
        o_attn = attn_fn(z3)
        o_conv, tail = _conv(z3, prefix, conv_w[l], conv_b[l], conv_ln_g[l], conv_ln_b[l], beta_conv[l],
                             d_attn, d_conv)
        h1 = _outproj(o_attn.reshape(b * s, d_attn), o_conv.reshape(b * s, d_conv), beta_attn[l], w_out_bf,
                      x, part(m, 2))
        return h1, z3, tail

    pad_pre = CONV_PAD_ROWS - (width - 1)
    zero_prefix = jnp.zeros((bp, CONV_PAD_ROWS, d_conv), F32)
    samp_prefix = jnp.pad(cache_conv[l], ((0, 0), (pad_pre, 0), (0, 0)))
    h1_p, z3_p, tail_p = mixer(x_prompt, mod_p, lambda z3: _attn_prompt(z3, tab_p, n_heads, head_dim, d_attn),
                               zero_prefix)
    h1_s, z3_s, tail_s = mixer(x_sample, mod_s,
                               lambda z3: _attn_sample(z3, cache_k[l], cache_v[l], tab_c, tab_n, n_heads,
                                                       head_dim, d_attn), samp_prefix)

    xp_p, ei_p, gk_p = _router(h1_p, part(mod_p, 3), part(mod_p, 4), g_ffn[l], wr_hi, wr_lo, router_bias[l])
    xp_s, ei_s, gk_s = _router(h1_s, part(mod_s, 3), part(mod_s, 4), g_ffn[l], wr_hi, wr_lo, router_bias[l])
    xp = jnp.concatenate([xp_p, xp_s], axis=0)
    eidx = jnp.concatenate([ei_p, ei_s], axis=1)
    gk = jnp.concatenate([gk_p, gk_s], axis=1)
    n_tok = xp.shape[0]
    tm = 256
    token_of_slot, slot_of_pair, tile_expert, n_tiles = _routing_tables(eidx, n_exp, tm)
    xs = _gather_rows(xp, token_of_slot)
    ys = _expert_ffn(xs, tile_expert, n_tiles, w_gate[l], w_up[l], w_down[l], tm)
    yu = _gather_rows(ys, slot_of_pair).reshape(TOP_K, n_tok, d // 2)
    ysh = _shared_ffn(xp, ws_gate[l].astype(BF16), ws_up[l].astype(BF16), ws_down[l].astype(BF16))
    gk_t = gk.T
    y_p = _combine(h1_p, part(mod_p, 5), ysh, yu, gk_t, 0)
    y_s = _combine(h1_s, part(mod_s, 5), ysh, yu, gk_t, bp * sp)

    def heads(z3, col0, keep):
        b, s, _ = z3.shape
        t = z3[:, s - keep:, col0:col0 + d_attn].astype(F32)
        return t.reshape(b, keep, n_heads, head_dim).transpose(0, 2, 1, 3)[None]

    keep_p = min(LEFT_REACH, sp)
    nw = width - 1
    return (y_p, y_s,
            heads(z3_p, d_attn, keep_p), heads(z3_p, 2 * d_attn, keep_p), tail_p[None, :, CONV_PAD_ROWS - nw:],
            heads(z3_s, d_attn, ss), heads(z3_s, 2 * d_attn, ss), tail_s[None, :, CONV_PAD_ROWS - nw:])
```

```python
import functools

import jax
import jax.numpy as jnp
from jax import lax
from jax.experimental import pallas as pl
from jax.experimental.pallas import tpu as pltpu

F32 = jnp.float32
BF16 = jnp.bfloat16
U32 = jnp.uint32
I32 = jnp.int32

CHUNK = 64
N_LEFT_CHUNKS = 8
LEFT_REACH = N_LEFT_CHUNKS * CHUNK
MAX_REL = 128
N_GROUPS = 8
TOPK_GROUPS = 4
TOP_K = 8
ROUTED_SCALE = 2.5
EPS = 1e-6
NEG_INF = -1e30

LANES = 128
SUBLANES = 8
CONV_PAD_ROWS = 32
V7X_VMEM_LIMIT = 56 * 1024 * 1024
EXPERT_TILE = 256


def _cparams(n_axes):
    return pltpu.CompilerParams(dimension_semantics=("arbitrary",) * n_axes,
                                vmem_limit_bytes=V7X_VMEM_LIMIT)


def _tile(n, pref):
    t = min(n, pref)
    while n % t:
        t //= 2
    return t


def _rows_block(batch, seq, rows):
    st = min(seq, rows)
    bt = rows // st
    assert bt * st == rows and batch % bt == 0 and seq % st == 0
    return bt, st


def _row_maps(bt, spb):
    if bt == 1:
        return (lambda i: (i // spb, i % spb)), (lambda i: i // spb)
    return (lambda i: (i, 0)), (lambda i: i)


def _pack_bf16_pair(a, b):
    au = lax.bitcast_convert_type(a.astype(BF16).astype(F32), U32)
    bu = lax.bitcast_convert_type(b.astype(BF16).astype(F32), U32)
    return au | (bu >> 16)


def _unpack_bf16_pair(p):
    a = lax.bitcast_convert_type(p & jnp.uint32(0xFFFF0000), F32)
    b = lax.bitcast_convert_type(p << 16, F32)
    return a, b


def _store_slabs(ref, val):
    n, w = val.shape
    slab = w // LANES
    for c in range(slab):
        ref[pl.ds(c, n, stride=slab), :] = val[:, c * LANES:(c + 1) * LANES]


def _load_slab_cols(ref, n, slab, c, row0=0):
    return ref[pl.ds(row0 * slab + c, n, stride=slab), :]


def _load_slabs(ref, n, slab):
    return jnp.concatenate([_load_slab_cols(ref, n, slab, c) for c in range(slab)], axis=1)


def _silu(x):
    return x * jax.nn.sigmoid(x)


def _ada_kernel(c_ref, w_ref, b_ref, o_ref):
    a = _silu(c_ref[...]).astype(BF16)
    o_ref[...] = jnp.dot(a, w_ref[...].astype(BF16), preferred_element_type=F32) + b_ref[...]


def _ada(c, w, b):
    m, d = c.shape
    n = w.shape[1]
    tn = _tile(n, 512)
    return pl.pallas_call(
        _ada_kernel,
        grid=(n // tn,),
        in_specs=[pl.BlockSpec((m, d), lambda j: (0, 0)),
                  pl.BlockSpec((d, tn), lambda j: (0, j)),
                  pl.BlockSpec((1, tn), lambda j: (0, j))],
        out_specs=pl.BlockSpec((m, tn), lambda j: (0, j)),
        out_shape=jax.ShapeDtypeStruct((m, n), F32),
        compiler_params=_cparams(1),
        name="ada",
    )(c, w, b.reshape(1, n))


def _modnorm(x, g, shift, scale):
    ms = jnp.mean(x * x, axis=-1, keepdims=True)
    y = x * lax.rsqrt(ms + EPS) * g
    return y * (1.0 + scale) + shift


def _inproj_kernel(x_ref, sh_ref, sc_ref, g_ref, w_ref, gh_ref, o_ref, hn_ref, *, n_norm_tiles, head_dim):
    j = pl.program_id(1)
    tm, d = hn_ref.shape
    tn = o_ref.shape[1]

    @pl.when(j == 0)
    def _():
        h = _modnorm(x_ref[...], g_ref[...], sh_ref[...], sc_ref[...])
        hn_ref[...] = h.reshape(tm, d).astype(BF16)

    acc = jnp.dot(hn_ref[...], w_ref[...], preferred_element_type=F32)

    @pl.when(j < n_norm_tiles)
    def _():
        for hh in range(tn // head_dim):
            sl = slice(hh * head_dim, (hh + 1) * head_dim)
            a = acc[:, sl]
            ms = jnp.mean(a * a, axis=-1, keepdims=True)
            o_ref[:, sl] = (a * lax.rsqrt(ms + EPS) * gh_ref[:, sl]).astype(BF16)

    @pl.when(j >= n_norm_tiles)
    def _():
        o_ref[...] = acc.astype(BF16)


def _inproj(x, shift, scale, g, w_bf, gh, head_dim):
    b, s, d = x.shape
    n = w_bf.shape[1]
    rows = b * s
    tm = _tile(rows, 512)
    tn = _tile(n, 1024)
    bt, st = _rows_block(b, s, tm)
    n_norm = gh.shape[1] // tn
    assert n_norm * tn == gh.shape[1] and tn % head_dim == 0
    rmap, bmap = _row_maps(bt, s // st)
    xmap = lambda i, j: (*rmap(i), 0)
    mmap = lambda i, j: (bmap(i), 0, 0)
    return pl.pallas_call(
        functools.partial(_inproj_kernel, n_norm_tiles=n_norm, head_dim=head_dim),
        grid=(rows // tm, n // tn),
        in_specs=[pl.BlockSpec((bt, st, d), xmap),
                  pl.BlockSpec((bt, 1, d), mmap),
                  pl.BlockSpec((bt, 1, d), mmap),
                  pl.BlockSpec((1, 1, d), lambda i, j: (0, 0, 0)),
                  pl.BlockSpec((d, tn), lambda i, j: (0, j)),
                  pl.BlockSpec((1, tn), lambda i, j: (0, jnp.minimum(j, n_norm - 1)))],
        out_specs=pl.BlockSpec((tm, tn), lambda i, j: (i, j)),
        out_shape=jax.ShapeDtypeStruct((rows, n), BF16),
        scratch_shapes=[pltpu.VMEM((tm, d), BF16)],
        compiler_params=_cparams(2),
        name="inproj",
    )(x, shift, scale, g.reshape(1, 1, d), w_bf, gh)


def _softmax_pv(s_list, v_list):
    m = functools.reduce(jnp.maximum, [jnp.max(s, axis=-1, keepdims=True) for s in s_list])
    p_list = [jnp.exp(s - m) for s in s_list]
    l = functools.reduce(jnp.add, [jnp.sum(p, axis=-1, keepdims=True) for p in p_list])
    o = functools.reduce(jnp.add, [jnp.dot(p.astype(BF16), v, preferred_element_type=F32)
                                   for p, v in zip(p_list, v_list)])
    return o / l


def _nt_dot(a, b):
    return lax.dot_general(a, b, (((1,), (1,)), ((), ())), preferred_element_type=F32)


def _attn_prompt_kernel(q_ref, k0_ref, k1_ref, k2_ref, v0_ref, v1_ref, v2_ref, t_ref, o_ref, *, scale, head_dim):
    i = pl.program_id(2)
    qb = q_ref.shape[1]
    for g in range(q_ref.shape[2] // head_dim):
        sl = slice(g * head_dim, (g + 1) * head_dim)
        q = q_ref[0, :, sl]
        k = jnp.concatenate([k0_ref[0, :, sl], k1_ref[0, :, sl], k2_ref[0, :, sl]], axis=0)
        v = jnp.concatenate([v0_ref[0, :, sl], v1_ref[0, :, sl], v2_ref[0, :, sl]], axis=0)
        s = _nt_dot(q, k) * scale + t_ref[g]
        col = lax.broadcasted_iota(I32, s.shape, 1)
        s = jnp.where(col < (2 - i) * qb, NEG_INF, s)
        o_ref[0, :, sl] = _softmax_pv([s], [v]).astype(BF16)


def _attn_prompt(z3, table, n_heads, head_dim, d_attn):
    b, s, _ = z3.shape
    qb = LEFT_REACH // 2
    hg = _tile(n_heads, 4)
    wg = hg * head_dim
    assert s % qb == 0 and table.shape == (n_heads, qb, 3 * qb)
    ck, cv = d_attn // wg, 2 * d_attn // wg

    def kv_spec(off, back):
        return pl.BlockSpec((1, qb, wg), lambda bb, h, i: (bb, jnp.maximum(i - back, 0), off + h))

    return pl.pallas_call(
        functools.partial(_attn_prompt_kernel, scale=head_dim ** -0.5, head_dim=head_dim),
        grid=(b, n_heads // hg, s // qb),
        in_specs=[pl.BlockSpec((1, qb, wg), lambda bb, h, i: (bb, i, h)),
                  kv_spec(ck, 2), kv_spec(ck, 1), kv_spec(ck, 0),
                  kv_spec(cv, 2), kv_spec(cv, 1), kv_spec(cv, 0),
                  pl.BlockSpec((hg, qb, 3 * qb), lambda bb, h, i: (h, 0, 0))],
        out_specs=pl.BlockSpec((1, qb, wg), lambda bb, h, i: (bb, i, h)),
        out_shape=jax.ShapeDtypeStruct((b, s, d_attn), BF16),
        compiler_params=_cparams(3),
        name="attn_prompt",
    )(z3, z3, z3, z3, z3, z3, z3, table)


def _attn_sample_kernel(q_ref, kn_ref, vn_ref, ck_ref, cv_ref, tc_ref, tn_ref, o_ref, *, scale, head_dim):
    for g in range(q_ref.shape[2] // head_dim):
        sl = slice(g * head_dim, (g + 1) * head_dim)
        q = q_ref[0, :, sl]
        ck = ck_ref[0, g].astype(BF16)
        cv = cv_ref[0, g].astype(BF16)
        s_c = _nt_dot(q, ck) * scale + tc_ref[g]
        s_n = _nt_dot(q, kn_ref[0, :, sl]) * scale + tn_ref[g]
        o_ref[0, :, sl] = _softmax_pv([s_c, s_n], [cv, vn_ref[0, :, sl]]).astype(BF16)


def _attn_sample(z3, cache_k, cache_v, tab_c, tab_n, n_heads, head_dim, d_attn):
    b, t, _ = z3.shape
    w = cache_k.shape[2]
    hg = _tile(n_heads, 8)
    wg = hg * head_dim
    ck, cv = d_attn // wg, 2 * d_attn // wg
    return pl.pallas_call(
        functools.partial(_attn_sample_kernel, scale=head_dim ** -0.5, head_dim=head_dim),
        grid=(b, n_heads // hg),
        in_specs=[pl.BlockSpec((1, t, wg), lambda bb, h: (bb, 0, h)),
                  pl.BlockSpec((1, t, wg), lambda bb, h: (bb, 0, ck + h)),
                  pl.BlockSpec((1, t, wg), lambda bb, h: (bb, 0, cv + h)),
                  pl.BlockSpec((1, hg, w, head_dim), lambda bb, h: (bb, h, 0, 0)),
                  pl.BlockSpec((1, hg, w, head_dim), lambda bb, h: (bb, h, 0, 0)),
                  pl.BlockSpec((hg, t, w), lambda bb, h: (h, 0, 0)),
                  pl.BlockSpec((hg, t, t), lambda bb, h: (h, 0, 0))],
        out_specs=pl.BlockSpec((1, t, wg), lambda bb, h: (bb, 0, h)),
        out_shape=jax.ShapeDtypeStruct((b, t, d_attn), BF16),
        compiler_params=_cparams(2),
        name="attn_sample",
    )(z3, z3, z3, cache_k, cache_v, tab_c, tab_n)


def _bias_tables(rel_bias, qb, t_new, past_win):
    def lookup(delta):
        return rel_bias[:, jnp.clip(delta, -MAX_REL, MAX_REL) + MAX_REL]

    r = jnp.arange(qb)[:, None]
    j = jnp.arange(3 * qb)[None, :]
    dchunk = j // CHUNK - r // CHUNK
    ok = (dchunk >= 0) & (dchunk <= N_LEFT_CHUNKS)
    tab_p = jnp.where(ok[None], lookup(j - 2 * qb - r), NEG_INF)
    rs = jnp.arange(t_new)[:, None]
    tab_c = lookup(jnp.arange(past_win)[None, :] - past_win - rs)
    tab_n = lookup(jnp.arange(t_new)[None, :] - rs)
    return tab_p.astype(F32), tab_c.astype(F32), tab_n.astype(F32)


def _conv_kernel(a_ref, g_ref, pre_ref, cw_ref, cb_ref, lg_ref, lb_ref, beta_ref, o_ref, tail_ref, buf_ref, y_ref,
                 *, width):
    t = pl.program_id(1)
    ts = a_ref.shape[1]
    rc, c_all = y_ref.shape
    pad = CONV_PAD_ROWS

    @pl.when(t == 0)
    def _():
        buf_ref[0:pad, :] = pre_ref[0]

    @pl.when(t > 0)
    def _():
        buf_ref[0:pad, :] = buf_ref[ts:ts + pad, :]

    buf_ref[pad:pad + ts, :] = a_ref[0].astype(F32) * jax.nn.sigmoid(g_ref[0].astype(F32))
    tail_ref[0] = buf_ref[ts:ts + pad, :]
    first = pad - (width - 1)
    taps_by_shift = [[w for w in range(width) if (first + w) % SUBLANES == b] for b in range(SUBLANES)]

    def chunk(ci, carry):
        base = pl.multiple_of(ci * rc, rc)

        def colblock(cb, carry2):
            col = pl.multiple_of(cb * LANES, LANES)
            win = buf_ref[pl.ds(base, rc + pad), pl.ds(col, LANES)]
            acc = None
            for b, taps in enumerate(taps_by_shift):
                if not taps:
                    continue
                sh = win if b == 0 else pltpu.roll(win, rc + pad - b, axis=0)
                for w in taps:
                    a8 = (first + w) // SUBLANES * SUBLANES
                    term = sh[a8:a8 + rc, :] * cw_ref[pl.ds(w, 1), pl.ds(col, LANES)]
                    acc = term if acc is None else acc + term
            y_ref[:, pl.ds(col, LANES)] = acc
            return carry2

        lax.fori_loop(0, c_all // LANES, colblock, 0)
        y = y_ref[...] + cb_ref[...]
        mu = jnp.mean(y, axis=-1, keepdims=True)
        yc = y - mu
        var = jnp.mean(yc * yc, axis=-1, keepdims=True)
        s = _silu(yc * lax.rsqrt(var + EPS) * lg_ref[...] + lb_ref[...])
        ms = jnp.mean(s * s, axis=-1, keepdims=True)
        o_ref[0, pl.ds(base, rc), :] = (s * lax.rsqrt(ms + EPS) * beta_ref[...]).astype(BF16)
        return carry

    lax.fori_loop(0, ts // rc, chunk, 0)


def _conv(z3, prefix, conv_w, conv_b, ln_g, ln_b, beta, d_attn, d_conv):
    b, t, _ = z3.shape
    width = conv_w.shape[0]
    ts = _tile(t, 256)
    rc = _tile(ts, 64)
    assert 3 * d_attn % d_conv == 0 and rc % 16 == 0 and ts >= CONV_PAD_ROWS and width - 1 <= CONV_PAD_ROWS
    ca = 3 * d_attn // d_conv
    vec = lambda x: x.reshape(1, d_conv)
    cst = lambda shape: pl.BlockSpec(shape, lambda bb, tt: (0,) * len(shape))
    return pl.pallas_call(
        functools.partial(_conv_kernel, width=width),
        grid=(b, t // ts),
        in_specs=[pl.BlockSpec((1, ts, d_conv), lambda bb, tt: (bb, tt, ca)),
                  pl.BlockSpec((1, ts, d_conv), lambda bb, tt: (bb, tt, ca + 1)),
                  pl.BlockSpec((1, CONV_PAD_ROWS, d_conv), lambda bb, tt: (bb, 0, 0)),
                  cst((width, d_conv)), cst((1, d_conv)), cst((1, d_conv)), cst((1, d_conv)), cst((1, d_conv))],
        out_specs=[pl.BlockSpec((1, ts, d_conv), lambda bb, tt: (bb, tt, 0)),
                   pl.BlockSpec((1, CONV_PAD_ROWS, d_conv), lambda bb, tt: (bb, 0, 0))],
        out_shape=[jax.ShapeDtypeStruct((b, t, d_conv), BF16),
                   jax.ShapeDtypeStruct((b, CONV_PAD_ROWS, d_conv), F32)],
        scratch_shapes=[pltpu.VMEM((CONV_PAD_ROWS + ts, d_conv), F32), pltpu.VMEM((rc, d_conv), F32)],
        compiler_params=_cparams(2),
        name="conv",
    )(z3, z3, prefix, conv_w, vec(conv_b), vec(ln_g), vec(ln_b), vec(beta))


def _outproj_kernel(oa_ref, oc_ref, ba_ref, w_ref, x_ref, gt_ref, o_ref, mx_ref):
    j = pl.program_id(1)
    da = oa_ref.shape[1]

    @pl.when(j == 0)
    def _():
        oa = oa_ref[...].astype(F32)
        ms = jnp.mean(oa * oa, axis=-1, keepdims=True)
        mx_ref[:, :da] = (oa * lax.rsqrt(ms + EPS) * ba_ref[...]).astype(BF16)
        mx_ref[:, da:] = oc_ref[...]

    acc = jnp.dot(mx_ref[...], w_ref[...], preferred_element_type=F32)
    o_ref[...] = x_ref[...] + gt_ref[...] * acc.reshape(x_ref.shape)


def _outproj(o_attn, o_conv, beta_attn, w_bf, x, gate):
    b, s, d = x.shape
    rows = b * s
    da, dc = o_attn.shape[1], o_conv.shape[1]
    tm = _tile(rows, 512)
    tn = _tile(d, 1024)
    bt, st = _rows_block(b, s, tm)
    rmap, bmap = _row_maps(bt, s // st)
    xmap = lambda i, j: (*rmap(i), j)
    gmap = lambda i, j: (bmap(i), 0, j)
    return pl.pallas_call(
        _outproj_kernel,
        grid=(rows // tm, d // tn),
        in_specs=[pl.BlockSpec((tm, da), lambda i, j: (i, 0)),
                  pl.BlockSpec((tm, dc), lambda i, j: (i, 0)),
                  pl.BlockSpec((1, da), lambda i, j: (0, 0)),
                  pl.BlockSpec((da + dc, tn), lambda i, j: (0, j)),
                  pl.BlockSpec((bt, st, tn), xmap),
                  pl.BlockSpec((bt, 1, tn), gmap)],
        out_specs=pl.BlockSpec((bt, st, tn), xmap),
        out_shape=jax.ShapeDtypeStruct((b, s, d), F32),
        scratch_shapes=[pltpu.VMEM((tm, da + dc), BF16)],
        compiler_params=_cparams(2),
        name="outproj",
    )(o_attn, o_conv, beta_attn.reshape(1, da), w_bf, x, gate)


def _router_kernel(ha_ref, hb_ref, sha_ref, sca_ref, shb_ref, scb_ref, g_ref, wh_ref, wl_ref, rb_ref, tri_ref,
                   xp_ref, ei_ref, gk_ref, pos_ref, cnt_ref, hbuf_ref, run_ref, *, n_a_tiles):
    i = pl.program_id(0)
    tm, d = hbuf_ref.shape
    dh = d // 2
    n_exp = wh_ref.shape[0]
    gsz = n_exp // N_GROUPS

    @pl.when(i == 0)
    def _():
        run_ref[...] = jnp.zeros(run_ref.shape, F32)

    @pl.when(i < n_a_tiles)
    def _():
        hbuf_ref[...] = _modnorm(ha_ref[...], g_ref[...], sha_ref[...], sca_ref[...]).reshape(tm, d)

    @pl.when(i >= n_a_tiles)
    def _():
        hbuf_ref[...] = _modnorm(hb_ref[...], g_ref[...], shb_ref[...], scb_ref[...]).reshape(tm, d)

    h = hbuf_ref[...]
    _store_slabs(xp_ref, _pack_bf16_pair(h[:, :dh], h[:, dh:]))
    hi = h.astype(BF16)
    lo = (h - hi.astype(F32)).astype(BF16)
    logits = _nt_dot(wh_ref[...], hi) + (_nt_dot(wh_ref[...], lo) + _nt_dot(wl_ref[...], hi))
    scores = jax.nn.sigmoid(logits)
    choice = scores + rb_ref[...]

    sub = lax.broadcasted_iota(I32, (gsz, tm), 0).astype(F32)
    blocks, gs_rows = [], []
    for g in range(N_GROUPS):
        blk = choice[g * gsz:(g + 1) * gsz, :]
        m1 = jnp.max(blk, axis=0, keepdims=True)
        first = jnp.min(jnp.where(blk == m1, sub, float(gsz)), axis=0, keepdims=True)
        m2 = jnp.max(jnp.where(sub == first, -jnp.inf, blk), axis=0, keepdims=True)
        blocks.append(blk)
        gs_rows.append(m1 + m2)
    gs = jnp.concatenate(gs_rows, axis=0)

    def rank_desc(x):
        rows = lax.broadcasted_iota(I32, x.shape, 0)
        rank = jnp.zeros(x.shape, I32)
        for r2 in range(x.shape[0]):
            other = x[r2:r2 + 1, :]
            ahead = jnp.where(other > x, 1, jnp.where((other == x) & (rows > r2), 1, 0))
            rank = rank + ahead
        return rank, rows

    grank, _ = rank_desc(gs)
    masked = jnp.concatenate(
        [jnp.where(grank[g:g + 1, :] < TOPK_GROUPS, blocks[g], NEG_INF) for g in range(N_GROUPS)], axis=0)
    erank, erow = rank_desc(masked)
    sel = erank < TOP_K
    w = jnp.where(sel, scores, 0.0)
    gates = w / jnp.sum(w, axis=0, keepdims=True) * ROUTED_SCALE

    sel_f = jnp.where(sel, 1.0, 0.0)
    incl = jnp.dot(sel_f.astype(BF16), tri_ref[...], preferred_element_type=F32)
    pos_e = run_ref[:, 0:1] + (incl - sel_f)
    run_ref[...] = run_ref[...] + incl[:, tm - 1:tm]
    cnt_ref[...] = run_ref[...]

    erow_f = erow.astype(F32)
    ei_rows, gk_rows, pos_rows = [], [], []
    for r in range(TOP_K):
        hit = erank == r
        ei_rows.append(jnp.sum(jnp.where(hit, erow_f, 0.0), axis=0, keepdims=True))
        gk_rows.append(jnp.sum(jnp.where(hit, gates, 0.0), axis=0, keepdims=True))
        pos_rows.append(jnp.sum(jnp.where(hit, pos_e, 0.0), axis=0, keepdims=True))
    ei_ref[...] = jnp.concatenate(ei_rows, axis=0).astype(I32)
    gk_ref[...] = jnp.concatenate(gk_rows, axis=0)
    pos_ref[...] = jnp.concatenate(pos_rows, axis=0).astype(I32)


def _router(h_a, mod_a, h_b, mod_b, g, wr_hi, wr_lo, rbias):
    ba, sa, d = h_a.shape
    bb, sb, _ = h_b.shape
    rows_a, rows_b = ba * sa, bb * sb
    n_exp = wr_hi.shape[0]
    tm = min(_tile(rows_a, 256), _tile(rows_b, 256))
    assert rows_a % tm == 0 and rows_b % tm == 0 and (d // 2) % LANES == 0
    slab = d // 2 // LANES
    nta, ntb = rows_a // tm, rows_b // tm
    bta, sta = _rows_block(ba, sa, tm)
    btb, stb = _rows_block(bb, sb, tm)
    rmap_a, bmap_a = _row_maps(bta, sa // sta)
    rmap_b, bmap_b = _row_maps(btb, sb // stb)
    ia = lambda i: jnp.minimum(i, nta - 1)
    ib = lambda i: jnp.maximum(i - nta, 0)
    tri = (jnp.arange(tm)[:, None] <= jnp.arange(tm)[None, :]).astype(BF16)
    n_tok = rows_a + rows_b
    cst2 = lambda shape: pl.BlockSpec(shape, lambda i: (0, 0))
    return pl.pallas_call(
        functools.partial(_router_kernel, n_a_tiles=nta),
        grid=(nta + ntb,),
        in_specs=[pl.BlockSpec((bta, sta, d), lambda i: (*rmap_a(ia(i)), 0)),
                  pl.BlockSpec((btb, stb, d), lambda i: (*rmap_b(ib(i)), 0)),
                  pl.BlockSpec((bta, 1, d), lambda i: (bmap_a(ia(i)), 0, 0)),
                  pl.BlockSpec((bta, 1, d), lambda i: (bmap_a(ia(i)), 0, 0)),
                  pl.BlockSpec((btb, 1, d), lambda i: (bmap_b(ib(i)), 0, 0)),
                  pl.BlockSpec((btb, 1, d), lambda i: (bmap_b(ib(i)), 0, 0)),
                  pl.BlockSpec((1, 1, d), lambda i: (0, 0, 0)),
                  cst2((n_exp, d)), cst2((n_exp, d)), cst2((n_exp, 1)), cst2((tm, tm))],
        out_specs=[pl.BlockSpec((tm * slab, LANES), lambda i: (i, 0)),
                   pl.BlockSpec((TOP_K, tm), lambda i: (0, i)),
                   pl.BlockSpec((TOP_K, tm), lambda i: (0, i)),
                   pl.BlockSpec((TOP_K, tm), lambda i: (0, i)),
                   cst2((n_exp, LANES))],
        out_shape=[jax.ShapeDtypeStruct((n_tok * slab, LANES), U32),
                   jax.ShapeDtypeStruct((TOP_K, n_tok), I32),
                   jax.ShapeDtypeStruct((TOP_K, n_tok), F32),
                   jax.ShapeDtypeStruct((TOP_K, n_tok), I32),
                   jax.ShapeDtypeStruct((n_exp, LANES), F32)],
        scratch_shapes=[pltpu.VMEM((tm, d), F32), pltpu.VMEM((n_exp, LANES), F32)],
        compiler_params=_cparams(1),
        name="router",
    )(h_a, h_b, mod_a[0], mod_a[1], mod_b[0], mod_b[1], g.reshape(1, 1, d), wr_hi, wr_lo,
      rbias.reshape(n_exp, 1), tri)


def _dispatch_kernel(lo_ref, hi_ref, slot_ref, x_ref, xs_ref, zero_ref, sem, *, n_tok_tiles, slab):
    i = pl.program_id(0)
    n_dma = slot_ref.shape[2]
    k = TOP_K
    tc = n_dma // k

    def slot_copy(src, slot):
        return pltpu.make_async_copy(src, xs_ref.at[pl.ds(pl.multiple_of(slot * slab, slab), slab)], sem)

    @pl.when(i < n_tok_tiles)
    def _():
        def start(n, carry):
            src = x_ref.at[pl.ds(pl.multiple_of(n * slab, slab), slab)]
            for r in range(k):
                slot_copy(src, slot_ref[0, 0, n * k + r]).start()
            return carry

        def wait(n, carry):
            for r in range(k):
                slot_copy(x_ref.at[pl.ds(0, slab)], 0).wait()
            return carry

        lax.fori_loop(0, tc, start, 0)
        lax.fori_loop(0, tc, wait, 0)

    @pl.when(i >= n_tok_tiles)
    def _():
        e = i - n_tok_tiles
        zero_ref[...] = jnp.zeros(zero_ref.shape, zero_ref.dtype)

        def start(s, carry):
            slot_copy(zero_ref, s).start()
            return carry

        def wait(s, carry):
            slot_copy(zero_ref, 0).wait()
            return carry

        lax.fori_loop(lo_ref[e], hi_ref[e], start, 0)
        lax.fori_loop(lo_ref[e], hi_ref[e], wait, 0)


def _dispatch(xp, slots, pad_lo, pad_hi, n_slots, slab):
    n_tok = xp.shape[0] // slab
    n_exp = pad_lo.shape[0]
    tc = _tile(n_tok, 128)
    ntt = n_tok // tc
    return pl.pallas_call(
        functools.partial(_dispatch_kernel, n_tok_tiles=ntt, slab=slab),
        grid_spec=pltpu.PrefetchScalarGridSpec(
            num_scalar_prefetch=2,
            grid=(ntt + n_exp,),
            in_specs=[pl.BlockSpec((1, 1, tc * TOP_K), lambda i, lo, hi: (jnp.minimum(i, ntt - 1), 0, 0),
                                   memory_space=pltpu.SMEM),
                      pl.BlockSpec((tc * slab, LANES), lambda i, lo, hi: (jnp.minimum(i, ntt - 1), 0))],
            out_specs=pl.BlockSpec(memory_space=pl.ANY),
            scratch_shapes=[pltpu.VMEM((slab, LANES), U32), pltpu.SemaphoreType.DMA(())]),
        out_shape=jax.ShapeDtypeStruct((n_slots * slab, LANES), U32),
        compiler_params=_cparams(1),
        name="dispatch",
    )(pad_lo, pad_hi, slots.reshape(ntt, 1, tc * TOP_K), xp)


def _expert_changed(te_ref, t):
    return (t == 0) | (te_ref[t] != te_ref[jnp.maximum(t - 1, 0)])


def _gateup_kernel(te_ref, nt_ref, x_ref, wg_ref, wu_ref, h_ref, wgb_ref, wub_ref, *, slab):
    t = pl.program_id(1)
    tm = h_ref.shape[0]
    dh = slab * LANES

    @pl.when(t < nt_ref[0])
    def _():
        @pl.when(_expert_changed(te_ref, t))
        def _():
            wgb_ref[...] = wg_ref[0].astype(BF16)
            wub_ref[...] = wu_ref[0].astype(BF16)

        xa, xb = _unpack_bf16_pair(_load_slabs(x_ref, tm, slab))
        xa, xb = xa.astype(BF16), xb.astype(BF16)
        g = (jnp.dot(xa, wgb_ref[:dh, :], preferred_element_type=F32)
             + jnp.dot(xb, wgb_ref[dh:, :], preferred_element_type=F32))
        u = (jnp.dot(xa, wub_ref[:dh, :], preferred_element_type=F32)
             + jnp.dot(xb, wub_ref[dh:, :], preferred_element_type=F32))
        h_ref[...] = (_silu(g) * u).astype(BF16)

    @pl.when(t >= nt_ref[0])
    def _():
        h_ref[...] = jnp.zeros(h_ref.shape, h_ref.dtype)


def _down_kernel(te_ref, nt_ref, h_ref, wd_ref, y_ref, wdb_ref):
    t = pl.program_id(0)
    dh = wd_ref.shape[2] // 2

    @pl.when(t < nt_ref[0])
    def _():
        @pl.when(_expert_changed(te_ref, t))
        def _():
            wdb_ref[...] = wd_ref[0].astype(BF16)

        y = jnp.dot(h_ref[...], wdb_ref[...], preferred_element_type=F32)
        _store_slabs(y_ref, _pack_bf16_pair(y[:, :dh], y[:, dh:]))

    @pl.when(t >= nt_ref[0])
    def _():
        y_ref[...] = jnp.zeros(y_ref.shape, y_ref.dtype)


def _expert_ffn(xs, tile_expert, n_tiles, w_gate, w_up, w_down, tm, slab):
    n_exp, d, f = w_gate.shape
    s_rows = xs.shape[0] // slab
    nt = s_rows // tm
    tf = _tile(f, 512)
    last = lambda t, ntr: jnp.minimum(t, ntr[0] - 1)
    hmid = pl.pallas_call(
        functools.partial(_gateup_kernel, slab=slab),
        grid_spec=pltpu.PrefetchScalarGridSpec(
            num_scalar_prefetch=2,
            grid=(f // tf, nt),
            in_specs=[pl.BlockSpec((tm * slab, LANES), lambda j, t, te, ntr: (last(t, ntr), 0)),
                      pl.BlockSpec((1, d, tf), lambda j, t, te, ntr: (te[t], 0, j)),
                      pl.BlockSpec((1, d, tf), lambda j, t, te, ntr: (te[t], 0, j))],
            out_specs=pl.BlockSpec((tm, tf), lambda j, t, te, ntr: (t, j)),
            scratch_shapes=[pltpu.VMEM((d, tf), BF16), pltpu.VMEM((d, tf), BF16)]),
        out_shape=jax.ShapeDtypeStruct((s_rows, f), BF16),
        compiler_params=_cparams(2),
        name="expert_gateup",
    )(tile_expert, n_tiles, xs, w_gate, w_up)
    return pl.pallas_call(
        _down_kernel,
        grid_spec=pltpu.PrefetchScalarGridSpec(
            num_scalar_prefetch=2,
            grid=(nt,),
            in_specs=[pl.BlockSpec((tm, f), lambda t, te, ntr: (last(t, ntr), 0)),
                      pl.BlockSpec((1, f, d), lambda t, te, ntr: (te[t], 0, 0))],
            out_specs=pl.BlockSpec((tm * slab, LANES), lambda t, te, ntr: (t, 0)),
            scratch_shapes=[pltpu.VMEM((f, d), BF16)]),
        out_shape=jax.ShapeDtypeStruct((s_rows * slab, LANES), U32),
        compiler_params=_cparams(1),
        name="expert_down",
    )(tile_expert, n_tiles, hmid, w_down)


def _shared_kernel(x_ref, wg_ref, wu_ref, wd_ref, o_ref, xa_ref, xb_ref, acc_ref, *, slab):
    j = pl.program_id(1)
    tm, dh = xa_ref.shape

    @pl.when(j == 0)
    def _():
        xa, xb = _unpack_bf16_pair(_load_slabs(x_ref, tm, slab))
        xa_ref[...] = xa.astype(BF16)
        xb_ref[...] = xb.astype(BF16)

    xa, xb = xa_ref[...], xb_ref[...]
    g = (jnp.dot(xa, wg_ref[:dh, :], preferred_element_type=F32)
         + jnp.dot(xb, wg_ref[dh:, :], preferred_element_type=F32))
    u = (jnp.dot(xa, wu_ref[:dh, :], preferred_element_type=F32)
         + jnp.dot(xb, wu_ref[dh:, :], preferred_element_type=F32))
    part = jnp.dot((_silu(g) * u).astype(BF16), wd_ref[...], preferred_element_type=F32)

    @pl.when(j == 0)
    def _():
        acc_ref[...] = part

    @pl.when(j > 0)
    def _():
        acc_ref[...] += part

    @pl.when(j == pl.num_programs(1) - 1)
    def _():
        o_ref[...] = acc_ref[...].astype(BF16)


def _shared_ffn(xp, wg_bf, wu_bf, wd_bf, slab):
    d, f = wg_bf.shape
    rows = xp.shape[0] // slab
    tm = _tile(rows, 512)
    tf = _tile(f, 256)
    return pl.pallas_call(
        functools.partial(_shared_kernel, slab=slab),
        grid=(rows // tm, f // tf),
        in_specs=[pl.BlockSpec((tm * slab, LANES), lambda i, j: (i, 0)),
                  pl.BlockSpec((d, tf), lambda i, j: (0, j)),
                  pl.BlockSpec((d, tf), lambda i, j: (0, j)),
                  pl.BlockSpec((tf, d), lambda i, j: (j, 0))],
        out_specs=pl.BlockSpec((tm, d), lambda i, j: (i, 0)),
        out_shape=jax.ShapeDtypeStruct((rows, d), BF16),
        scratch_shapes=[pltpu.VMEM((tm, d // 2), BF16), pltpu.VMEM((tm, d // 2), BF16), pltpu.VMEM((tm, d), F32)],
        compiler_params=_cparams(2),
        name="shared_ffn",
    )(xp, wg_bf, wu_bf, wd_bf)


def _combine_kernel(slot_ref, h_ref, gt_ref, ys_ref, gk_ref, y_hbm, o_ref, buf_ref, sem, *, slab):
    tc, d = ys_ref.shape
    dh = d // 2
    k = TOP_K

    def slab_copy(slot, n, r):
        return pltpu.make_async_copy(y_hbm.at[pl.ds(pl.multiple_of(slot * slab, slab), slab)],
                                     buf_ref.at[pl.ds(pl.multiple_of((r * tc + n) * slab, slab), slab)], sem)

    def start(n, carry):
        for r in range(k):
            slab_copy(slot_ref[0, 0, n * k + r], n, r).start()
        return carry

    def wait(n, carry):
        for r in range(k):
            slab_copy(0, 0, 0).wait()
        return carry

    lax.fori_loop(0, tc, start, 0)
    lax.fori_loop(0, tc, wait, 0)

    shp = (h_ref.shape[0], h_ref.shape[1], LANES)
    gks = [gk_ref[:, r:r + 1] for r in range(k)]
    for c in range(slab):
        ca = slice(c * LANES, (c + 1) * LANES)
        cb = slice(dh + c * LANES, dh + (c + 1) * LANES)
        acc_a = ys_ref[:, ca].astype(F32)
        acc_b = ys_ref[:, cb].astype(F32)
        for r in range(k):
            ya, yb = _unpack_bf16_pair(_load_slab_cols(buf_ref, tc, slab, c, row0=r * tc))
            acc_a = acc_a + gks[r] * ya
            acc_b = acc_b + gks[r] * yb
        o_ref[:, :, ca] = h_ref[:, :, ca] + gt_ref[:, :, ca] * acc_a.reshape(shp)
        o_ref[:, :, cb] = h_ref[:, :, cb] + gt_ref[:, :, cb] * acc_b.reshape(shp)


def _combine(h1, gate, ysh, ys, slots, gk, row0, slab):
    b, s, d = h1.shape
    rows = b * s
    tc = _tile(rows, 64)
    bt, st = _rows_block(b, s, tc)
    assert row0 % tc == 0
    t0 = row0 // tc
    rmap, bmap = _row_maps(bt, s // st)
    xmap = lambda i: (*rmap(i), 0)
    gmap = lambda i: (bmap(i), 0, 0)
    n_tok = gk.shape[0]
    return pl.pallas_call(
        functools.partial(_combine_kernel, slab=slab),
        grid=(rows // tc,),
        in_specs=[pl.BlockSpec((1, 1, tc * TOP_K), lambda i: (t0 + i, 0, 0), memory_space=pltpu.SMEM),
                  pl.BlockSpec((bt, st, d), xmap),
                  pl.BlockSpec((bt, 1, d), gmap),
                  pl.BlockSpec((tc, d), lambda i: (t0 + i, 0)),
                  pl.BlockSpec((tc, TOP_K), lambda i: (t0 + i, 0)),
                  pl.BlockSpec(memory_space=pl.ANY)],
        out_specs=pl.BlockSpec((bt, st, d), xmap),
        out_shape=jax.ShapeDtypeStruct((b, s, d), F32),
        scratch_shapes=[pltpu.VMEM((TOP_K * tc * slab, LANES), U32), pltpu.SemaphoreType.DMA(())],
        compiler_params=_cparams(1),
        name="combine",
    )(slots.reshape(n_tok // tc, 1, tc * TOP_K), h1, gate, ysh, gk, ys)


def _slot_tables(counts, eidx, pos, tm):
    n_exp = counts.shape[0]
    k, n = eidx.shape
    nt_max = (k * n) // tm + n_exp
    tiles = (counts + tm - 1) // tm
    tile_end = jnp.cumsum(tiles)
    off = (tile_end - tiles) * tm
    n_tiles = tile_end[-1:].astype(I32)
    t_ar = jnp.arange(nt_max, dtype=I32)
    te = jnp.minimum(jnp.sum((tile_end[None, :] <= t_ar[:, None]).astype(I32), axis=1), n_exp - 1)
    te_last = jnp.sum(jnp.where(t_ar == n_tiles[0] - 1, te, 0))
    tile_expert = jnp.where(t_ar < n_tiles[0], te, te_last).astype(I32)
    onehot = eidx[None, :, :] == jnp.arange(n_exp, dtype=I32)[:, None, None]
    slot = pos + jnp.sum(jnp.where(onehot, off[:, None, None], 0), axis=0)
    pad_lo = (off + counts).astype(I32)
    pad_hi = jnp.concatenate([off[1:], jnp.full((1,), nt_max * tm, off.dtype)]).astype(I32)
    return slot.T.reshape(n * k).astype(I32), tile_expert, n_tiles, pad_lo, pad_hi, nt_max * tm


def kernel(x_prompt, x_sample, c_prompt, c_sample, cache_k, cache_v, cache_conv, w_ada, b_ada, g_mix, w_in, g_q, g_k, rel_bias, conv_w, conv_b, conv_ln_g, conv_ln_b, beta_attn, beta_conv, w_out, g_ffn, w_router, router_bias, w_gate, w_up, w_down, ws_gate, ws_up, ws_down):
    depth = w_in.shape[0]
    assert depth == 1
    bp, sp, d = x_prompt.shape
    bs, ss, _ = x_sample.shape
    head_dim = g_q.shape[1]
    n_heads = rel_bias.shape[1]
    d_attn = n_heads * head_dim
    d_conv = conv_w.shape[2]
    width = conv_w.shape[1]
    n_exp = w_router.shape[2]
    past_win = cache_k.shape[3]
    l = 0

    n_c = bp + bs
    c_all = jnp.concatenate([c_prompt, c_sample], axis=0)
    c_all = jnp.pad(c_all, ((0, -n_c % 16), (0, 0)))
    mod = _ada(c_all, w_ada[l], b_ada[l])
    mod_p = mod[:bp].reshape(bp, 1, 6 * d)
    mod_s = mod[bp:n_c].reshape(bs, 1, 6 * d)
    part = lambda m, i: m[:, :, i * d:(i + 1) * d]

    w_in_bf = w_in[l].astype(BF16)
    w_out_bf = w_out[l].astype(BF16)
    gh = jnp.concatenate([jnp.tile(g_q[l], n_heads), jnp.tile(g_k[l], n_heads)]).reshape(1, 2 * d_attn)
    tab_p, tab_c, tab_n = _bias_tables(rel_bias[l], LEFT_REACH // 2, ss, past_win)
    wr_t = w_router[l].T
    wr_hi = wr_t.astype(BF16)
    wr_lo = (wr_t - wr_hi.astype(F32)).astype(BF16)

    def mixer(x, m, attn_fn, prefix):
        b, s, _ = x.shape
        z = _inproj(x, part(m, 0), part(m, 1), g_mix[l], w_in_bf, gh, head_dim)
        z3 = z.reshape(b, s, z.shape[1])
        o_attn = attn_fn(z3)
        o_conv, tail = _conv(z3, prefix, conv_w[l], conv_b[l], conv_ln_g[l], conv_ln_b[l], beta_conv[l],
                             d_attn, d_conv)
        h1 = _outproj(o_attn.reshape(b * s, d_attn), o_conv.reshape(b * s, d_conv), beta_attn[l], w_out_bf,
                      x, part(m, 2))
        return h1, z3, tail

    pad_pre = CONV_PAD_ROWS - (width - 1)
    zero_prefix = jnp.zeros((bp, CONV_PAD_ROWS, d_conv), F32)
    samp_prefix = jnp.pad(cache_conv[l], ((0, 0), (pad_pre, 0), (0, 0)))
    h1_p, z3_p, tail_p = mixer(x_prompt, mod_p, lambda z3: _attn_prompt(z3, tab_p, n_heads, head_dim, d_attn),
                               zero_prefix)
    h1_s, z3_s, tail_s = mixer(x_sample, mod_s,
                               lambda z3: _attn_sample(z3, cache_k[l], cache_v[l], tab_c, tab_n, n_heads,
                                                       head_dim, d_attn), samp_prefix)

    slab = d // 2 // LANES
    xp, eidx, gk, pos, cnt = _router(h1_p, (part(mod_p, 3), part(mod_p, 4)), h1_s, (part(mod_s, 3), part(mod_s, 4)),
                                     g_ffn[l], wr_hi, wr_lo, router_bias[l])
    counts = cnt[:, 0].astype(I32)
    slots, tile_expert, n_tiles, pad_lo, pad_hi, n_slots = _slot_tables(counts, eidx, pos, EXPERT_TILE)
    xs = _dispatch(xp, slots, pad_lo, pad_hi, n_slots, slab)
    ys = _expert_ffn(xs, tile_expert, n_tiles, w_gate[l], w_up[l], w_down[l], EXPERT_TILE, slab)
    ysh = _shared_ffn(xp, ws_gate[l].astype(BF16), ws_up[l].astype(BF16), ws_down[l].astype(BF16), slab)
    gk_t = gk.T
    y_p = _combine(h1_p, part(mod_p, 5), ysh, ys, slots, gk_t, 0, slab)
    y_s = _combine(h1_s, part(mod_s, 5), ysh, ys, slots, gk_t, bp * sp, slab)

    def heads(z3, col0, keep):
        b, s, _ = z3.shape
        t = z3[:, s - keep:, col0:col0 + d_attn].astype(F32)
        return t.reshape(b, keep, n_heads, head_dim).transpose(0, 2, 1, 3)[None]

    keep_p = min(LEFT_REACH, sp)
    nw = width - 1
    return (y_p, y_s,
            heads(z3_p, d_attn, keep_p), heads(z3_p, 2 * d_attn, keep_p), tail_p[None, :, CONV_PAD_ROWS - nw:],
            heads(z3_s, d_attn, ss), heads(z3_s, 2 * d_attn, ss), tail_s[None, :, CONV_PAD_ROWS - nw:])
```

```python
import functools

import jax
import jax.numpy as jnp
from jax import lax
from jax.experimental import pallas as pl
from jax.experimental.pallas import tpu as pltpu

F32 = jnp.float32
BF16 = jnp.bfloat16
U32 = jnp.uint32
I32 = jnp.int32

CHUNK = 64
N_LEFT_CHUNKS = 8
LEFT_REACH = N_LEFT_CHUNKS * CHUNK
MAX_REL = 128
N_GROUPS = 8
TOPK_GROUPS = 4
TOP_K = 8
ROUTED_SCALE = 2.5
EPS = 1e-6
NEG_INF = -1e30

LANES = 128
SUBLANES = 8
CONV_PAD_ROWS = 32
V7X_VMEM_LIMIT = 56 * 1024 * 1024
EXPERT_TILE = 512


def _cparams(n_axes):
    return pltpu.CompilerParams(dimension_semantics=("arbitrary",) * n_axes,
                                vmem_limit_bytes=V7X_VMEM_LIMIT)


def _tile(n, pref):
    t = min(n, pref)
    while n % t:
        t //= 2
    return t


def _rows_block(batch, seq, rows):
    st = min(seq, rows)
    bt = rows // st
    assert bt * st == rows and batch % bt == 0 and seq % st == 0
    return bt, st


def _row_maps(bt, spb):
    if bt == 1:
        return (lambda i: (i // spb, i % spb)), (lambda i: i // spb)
    return (lambda i: (i, 0)), (lambda i: i)


def _pack_bf16_pair(a, b):
    au = lax.bitcast_convert_type(a.astype(BF16).astype(F32), U32)
    bu = lax.bitcast_convert_type(b.astype(BF16).astype(F32), U32)
    return au | (bu >> 16)


def _unpack_bf16_pair(p):
    a = lax.bitcast_convert_type(p & jnp.uint32(0xFFFF0000), F32)
    b = lax.bitcast_convert_type(p << 16, F32)
    return a, b


def _store_slabs(ref, val):
    n, w = val.shape
    slab = w // LANES
    for c in range(slab):
        ref[pl.ds(c, n, stride=slab), :] = val[:, c * LANES:(c + 1) * LANES]


def _load_slab_cols(ref, n, slab, c, row0=0):
    return ref[pl.ds(row0 * slab + c, n, stride=slab), :]


def _load_slabs(ref, n, slab):
    return jnp.concatenate([_load_slab_cols(ref, n, slab, c) for c in range(slab)], axis=1)


def _silu(x):
    return x * jax.nn.sigmoid(x)


def _ada_kernel(c_ref, w_ref, b_ref, o_ref):
    a = _silu(c_ref[...]).astype(BF16)
    o_ref[...] = jnp.dot(a, w_ref[...].astype(BF16), preferred_element_type=F32) + b_ref[...]


def _ada(c, w, b):
    m, d = c.shape
    n = w.shape[1]
    tn = _tile(n, 512)
    return pl.pallas_call(
        _ada_kernel,
        grid=(n // tn,),
        in_specs=[pl.BlockSpec((m, d), lambda j: (0, 0)),
                  pl.BlockSpec((d, tn), lambda j: (0, j)),
                  pl.BlockSpec((1, tn), lambda j: (0, j))],
        out_specs=pl.BlockSpec((m, tn), lambda j: (0, j)),
        out_shape=jax.ShapeDtypeStruct((m, n), F32),
        compiler_params=_cparams(1),
        name="ada",
    )(c, w, b.reshape(1, n))


def _modnorm(x, g, shift, scale):
    ms = jnp.mean(x * x, axis=-1, keepdims=True)
    y = x * lax.rsqrt(ms + EPS) * g
    return y * (1.0 + scale) + shift


def _inproj_kernel(x_ref, sh_ref, sc_ref, g_ref, w_ref, gh_ref, o_ref, hn_ref, *, n_norm_tiles, head_dim):
    j = pl.program_id(1)
    tm, d = hn_ref.shape
    tn = o_ref.shape[1]

    @pl.when(j == 0)
    def _():
        h = _modnorm(x_ref[...], g_ref[...], sh_ref[...], sc_ref[...])
        hn_ref[...] = h.reshape(tm, d).astype(BF16)

    acc = jnp.dot(hn_ref[...], w_ref[...], preferred_element_type=F32)

    @pl.when(j < n_norm_tiles)
    def _():
        for hh in range(tn // head_dim):
            sl = slice(hh * head_dim, (hh + 1) * head_dim)
            a = acc[:, sl]
            ms = jnp.mean(a * a, axis=-1, keepdims=True)
            o_ref[:, sl] = (a * lax.rsqrt(ms + EPS) * gh_ref[:, sl]).astype(BF16)

    @pl.when(j >= n_norm_tiles)
    def _():
        o_ref[...] = acc.astype(BF16)


def _inproj(x, shift, scale, g, w_bf, gh, head_dim):
    b, s, d = x.shape
    n = w_bf.shape[1]
    rows = b * s
    tm = _tile(rows, 512)
    tn = _tile(n, 1024)
    bt, st = _rows_block(b, s, tm)
    n_norm = gh.shape[1] // tn
    assert n_norm * tn == gh.shape[1] and tn % head_dim == 0
    rmap, bmap = _row_maps(bt, s // st)
    xmap = lambda i, j: (*rmap(i), 0)
    mmap = lambda i, j: (bmap(i), 0, 0)
    return pl.pallas_call(
        functools.partial(_inproj_kernel, n_norm_tiles=n_norm, head_dim=head_dim),
        grid=(rows // tm, n // tn),
        in_specs=[pl.BlockSpec((bt, st, d), xmap),
                  pl.BlockSpec((bt, 1, d), mmap),
                  pl.BlockSpec((bt, 1, d), mmap),
                  pl.BlockSpec((1, 1, d), lambda i, j: (0, 0, 0)),
                  pl.BlockSpec((d, tn), lambda i, j: (0, j)),
                  pl.BlockSpec((1, tn), lambda i, j: (0, jnp.minimum(j, n_norm - 1)))],
        out_specs=pl.BlockSpec((tm, tn), lambda i, j: (i, j)),
        out_shape=jax.ShapeDtypeStruct((rows, n), BF16),
        scratch_shapes=[pltpu.VMEM((tm, d), BF16)],
        compiler_params=_cparams(2),
        name="inproj",
    )(x, shift, scale, g.reshape(1, 1, d), w_bf, gh)


def _softmax_pv(s_list, v_list):
    m = functools.reduce(jnp.maximum, [jnp.max(s, axis=-1, keepdims=True) for s in s_list])
    p_list = [jnp.exp(s - m) for s in s_list]
    l = functools.reduce(jnp.add, [jnp.sum(p, axis=-1, keepdims=True) for p in p_list])
    o = functools.reduce(jnp.add, [jnp.dot(p.astype(BF16), v, preferred_element_type=F32)
                                   for p, v in zip(p_list, v_list)])
    return o / l


def _nt_dot(a, b):
    return lax.dot_general(a, b, (((1,), (1,)), ((), ())), preferred_element_type=F32)


def _attn_prompt_kernel(q_ref, k0_ref, k1_ref, k2_ref, v0_ref, v1_ref, v2_ref, t_ref, o_ref, *, scale, head_dim):
    i = pl.program_id(2)
    qb = q_ref.shape[1]
    for g in range(q_ref.shape[2] // head_dim):
        sl = slice(g * head_dim, (g + 1) * head_dim)
        q = q_ref[0, :, sl]
        k = jnp.concatenate([k0_ref[0, :, sl], k1_ref[0, :, sl], k2_ref[0, :, sl]], axis=0)
        v = jnp.concatenate([v0_ref[0, :, sl], v1_ref[0, :, sl], v2_ref[0, :, sl]], axis=0)
        s = _nt_dot(q, k) * scale + t_ref[g]
        col = lax.broadcasted_iota(I32, s.shape, 1)
        s = jnp.where(col < (2 - i) * qb, NEG_INF, s)
        o_ref[0, :, sl] = _softmax_pv([s], [v]).astype(BF16)


def _attn_prompt(z3, table, n_heads, head_dim, d_attn):
    b, s, _ = z3.shape
    qb = LEFT_REACH // 2
    hg = _tile(n_heads, 4)
    wg = hg * head_dim
    assert s % qb == 0 and table.shape == (n_heads, qb, 3 * qb)
    ck, cv = d_attn // wg, 2 * d_attn // wg

    def kv_spec(off, back):
        return pl.BlockSpec((1, qb, wg), lambda bb, h, i: (bb, jnp.maximum(i - back, 0), off + h))

    return pl.pallas_call(
        functools.partial(_attn_prompt_kernel, scale=head_dim ** -0.5, head_dim=head_dim),
        grid=(b, n_heads // hg, s // qb),
        in_specs=[pl.BlockSpec((1, qb, wg), lambda bb, h, i: (bb, i, h)),
                  kv_spec(ck, 2), kv_spec(ck, 1), kv_spec(ck, 0),
                  kv_spec(cv, 2), kv_spec(cv, 1), kv_spec(cv, 0),
                  pl.BlockSpec((hg, qb, 3 * qb), lambda bb, h, i: (h, 0, 0))],
        out_specs=pl.BlockSpec((1, qb, wg), lambda bb, h, i: (bb, i, h)),
        out_shape=jax.ShapeDtypeStruct((b, s, d_attn), BF16),
        compiler_params=_cparams(3),
        name="attn_prompt",
    )(z3, z3, z3, z3, z3, z3, z3, table)


def _attn_sample_kernel(q_ref, kn_ref, vn_ref, ck_ref, cv_ref, tc_ref, tn_ref, o_ref, *, scale, head_dim):
    for g in range(q_ref.shape[2] // head_dim):
        sl = slice(g * head_dim, (g + 1) * head_dim)
        q = q_ref[0, :, sl]
        ck = ck_ref[0, g].astype(BF16)
        cv = cv_ref[0, g].astype(BF16)
        s_c = _nt_dot(q, ck) * scale + tc_ref[g]
        s_n = _nt_dot(q, kn_ref[0, :, sl]) * scale + tn_ref[g]
        o_ref[0, :, sl] = _softmax_pv([s_c, s_n], [cv, vn_ref[0, :, sl]]).astype(BF16)


def _attn_sample(z3, cache_k, cache_v, tab_c, tab_n, n_heads, head_dim, d_attn):
    b, t, _ = z3.shape
    w = cache_k.shape[2]
    hg = _tile(n_heads, 8)
    wg = hg * head_dim
    ck, cv = d_attn // wg, 2 * d_attn // wg
    return pl.pallas_call(
        functools.partial(_attn_sample_kernel, scale=head_dim ** -0.5, head_dim=head_dim),
        grid=(b, n_heads // hg),
        in_specs=[pl.BlockSpec((1, t, wg), lambda bb, h: (bb, 0, h)),
                  pl.BlockSpec((1, t, wg), lambda bb, h: (bb, 0, ck + h)),
                  pl.BlockSpec((1, t, wg), lambda bb, h: (bb, 0, cv + h)),
                  pl.BlockSpec((1, hg, w, head_dim), lambda bb, h: (bb, h, 0, 0)),
                  pl.BlockSpec((1, hg, w, head_dim), lambda bb, h: (bb, h, 0, 0)),
                  pl.BlockSpec((hg, t, w), lambda bb, h: (h, 0, 0)),
                  pl.BlockSpec((hg, t, t), lambda bb, h: (h, 0, 0))],
        out_specs=pl.BlockSpec((1, t, wg), lambda bb, h: (bb, 0, h)),
        out_shape=jax.ShapeDtypeStruct((b, t, d_attn), BF16),
        compiler_params=_cparams(2),
        name="attn_sample",
    )(z3, z3, z3, cache_k, cache_v, tab_c, tab_n)


def _rel_bias_tile(rel_bias, n_rows, n_cols, delta0):
    col0 = n_rows - 1
    length = col0 + n_cols + 1
    idx = jnp.clip(jnp.arange(length) - col0 + delta0, -MAX_REL, MAX_REL) + MAX_REL
    ext = rel_bias[:, idx]
    h = ext.shape[0]
    skew = jnp.tile(ext, (1, n_rows))[:, :n_rows * (length - 1)].reshape(h, n_rows, length - 1)
    return skew[:, :, col0:col0 + n_cols].astype(F32)


def _bias_tables(rel_bias, qb, t_new, past_win):
    r = jnp.arange(qb)[:, None]
    j = jnp.arange(3 * qb)[None, :]
    dchunk = j // CHUNK - r // CHUNK
    ok = (dchunk >= 0) & (dchunk <= N_LEFT_CHUNKS)
    tab_p = jnp.where(ok[None], _rel_bias_tile(rel_bias, qb, 3 * qb, -2 * qb), NEG_INF)
    tab_c = _rel_bias_tile(rel_bias, t_new, past_win, -past_win)
    tab_n = _rel_bias_tile(rel_bias, t_new, t_new, 0)
    return tab_p, tab_c, tab_n


def _conv_kernel(a_ref, g_ref, pre_ref, cw_ref, cb_ref, lg_ref, lb_ref, beta_ref, o_ref, tail_ref, buf_ref, y_ref,
                 *, width):
    t = pl.program_id(1)
    ts = a_ref.shape[1]
    rc, c_all = y_ref.shape
    pad = CONV_PAD_ROWS

    @pl.when(t == 0)
    def _():
        buf_ref[0:pad, :] = pre_ref[0]

    @pl.when(t > 0)
    def _():
        buf_ref[0:pad, :] = buf_ref[ts:ts + pad, :]

    buf_ref[pad:pad + ts, :] = a_ref[0].astype(F32) * jax.nn.sigmoid(g_ref[0].astype(F32))
    tail_ref[0] = buf_ref[ts:ts + pad, :]
    first = pad - (width - 1)
    taps_by_shift = [[w for w in range(width) if (first + w) % SUBLANES == b] for b in range(SUBLANES)]

    def chunk(ci, carry):
        base = pl.multiple_of(ci * rc, rc)

        def colblock(cb, carry2):
            col = pl.multiple_of(cb * LANES, LANES)
            win = buf_ref[pl.ds(base, rc + pad), pl.ds(col, LANES)]
            acc = None
            for b, taps in enumerate(taps_by_shift):
                if not taps:
                    continue
                sh = win if b == 0 else pltpu.roll(win, rc + pad - b, axis=0)
                for w in taps:
                    a8 = (first + w) // SUBLANES * SUBLANES
                    term = sh[a8:a8 + rc, :] * cw_ref[pl.ds(w, 1), pl.ds(col, LANES)]
                    acc = term if acc is None else acc + term
            y_ref[:, pl.ds(col, LANES)] = acc
            return carry2

        lax.fori_loop(0, c_all // LANES, colblock, 0)
        y = y_ref[...] + cb_ref[...]
        mu = jnp.mean(y, axis=-1, keepdims=True)
        yc = y - mu
        var = jnp.mean(yc * yc, axis=-1, keepdims=True)
        s = _silu(yc * lax.rsqrt(var + EPS) * lg_ref[...] + lb_ref[...])
        ms = jnp.mean(s * s, axis=-1, keepdims=True)
        o_ref[0, pl.ds(base, rc), :] = (s * lax.rsqrt(ms + EPS) * beta_ref[...]).astype(BF16)
        return carry

    lax.fori_loop(0, ts // rc, chunk, 0)


def _conv(z3, prefix, conv_w, conv_b, ln_g, ln_b, beta, d_attn, d_conv):
    b, t, _ = z3.shape
    width = conv_w.shape[0]
    ts = _tile(t, 256)
    rc = _tile(ts, 64)
    assert 3 * d_attn % d_conv == 0 and rc % 16 == 0 and ts >= CONV_PAD_ROWS and width - 1 <= CONV_PAD_ROWS
    ca = 3 * d_attn // d_conv
    vec = lambda x: x.reshape(1, d_conv)
    cst = lambda shape: pl.BlockSpec(shape, lambda bb, tt: (0,) * len(shape))
    return pl.pallas_call(
        functools.partial(_conv_kernel, width=width),
        grid=(b, t // ts),
        in_specs=[pl.BlockSpec((1, ts, d_conv), lambda bb, tt: (bb, tt, ca)),
                  pl.BlockSpec((1, ts, d_conv), lambda bb, tt: (bb, tt, ca + 1)),
                  pl.BlockSpec((1, CONV_PAD_ROWS, d_conv), lambda bb, tt: (bb, 0, 0)),
                  cst((width, d_conv)), cst((1, d_conv)), cst((1, d_conv)), cst((1, d_conv)), cst((1, d_conv))],
        out_specs=[pl.BlockSpec((1, ts, d_conv), lambda bb, tt: (bb, tt, 0)),
                   pl.BlockSpec((1, CONV_PAD_ROWS, d_conv), lambda bb, tt: (bb, 0, 0))],
        out_shape=[jax.ShapeDtypeStruct((b, t, d_conv), BF16),
                   jax.ShapeDtypeStruct((b, CONV_PAD_ROWS, d_conv), F32)],
        scratch_shapes=[pltpu.VMEM((CONV_PAD_ROWS + ts, d_conv), F32), pltpu.VMEM((rc, d_conv), F32)],
        compiler_params=_cparams(2),
        name="conv",
    )(z3, z3, prefix, conv_w, vec(conv_b), vec(ln_g), vec(ln_b), vec(beta))


def _outproj_kernel(oa_ref, oc_ref, ba_ref, w_ref, x_ref, gt_ref, o_ref, mx_ref):
    j = pl.program_id(1)
    da = oa_ref.shape[1]

    @pl.when(j == 0)
    def _():
        oa = oa_ref[...].astype(F32)
        ms = jnp.mean(oa * oa, axis=-1, keepdims=True)
        mx_ref[:, :da] = (oa * lax.rsqrt(ms + EPS) * ba_ref[...]).astype(BF16)
        mx_ref[:, da:] = oc_ref[...]

    acc = jnp.dot(mx_ref[...], w_ref[...], preferred_element_type=F32)
    o_ref[...] = x_ref[...] + gt_ref[...] * acc.reshape(x_ref.shape)


def _outproj(o_attn, o_conv, beta_attn, w_bf, x, gate):
    b, s, d = x.shape
    rows = b * s
    da, dc = o_attn.shape[1], o_conv.shape[1]
    tm = _tile(rows, 512)
    tn = _tile(d, 1024)
    bt, st = _rows_block(b, s, tm)
    rmap, bmap = _row_maps(bt, s // st)
    xmap = lambda i, j: (*rmap(i), j)
    gmap = lambda i, j: (bmap(i), 0, j)
    return pl.pallas_call(
        _outproj_kernel,
        grid=(rows // tm, d // tn),
        in_specs=[pl.BlockSpec((tm, da), lambda i, j: (i, 0)),
                  pl.BlockSpec((tm, dc), lambda i, j: (i, 0)),
                  pl.BlockSpec((1, da), lambda i, j: (0, 0)),
                  pl.BlockSpec((da + dc, tn), lambda i, j: (0, j)),
                  pl.BlockSpec((bt, st, tn), xmap),
                  pl.BlockSpec((bt, 1, tn), gmap)],
        out_specs=pl.BlockSpec((bt, st, tn), xmap),
        out_shape=jax.ShapeDtypeStruct((b, s, d), F32),
        scratch_shapes=[pltpu.VMEM((tm, da + dc), BF16)],
        compiler_params=_cparams(2),
        name="outproj",
    )(o_attn, o_conv, beta_attn.reshape(1, da), w_bf, x, gate)


def _router_kernel(ha_ref, hb_ref, sha_ref, sca_ref, shb_ref, scb_ref, g_ref, wh_ref, wl_ref, rb_ref, tri_ref,
                   xp_ref, ei_ref, gk_ref, pos_ref, cnt_ref, hbuf_ref, run_ref, *, n_a_tiles):
    i = pl.program_id(0)
    tm, d = hbuf_ref.shape
    dh = d // 2
    n_exp = wh_ref.shape[0]
    gsz = n_exp // N_GROUPS

    @pl.when(i == 0)
    def _():
        run_ref[...] = jnp.zeros(run_ref.shape, F32)

    @pl.when(i < n_a_tiles)
    def _():
        hbuf_ref[...] = _modnorm(ha_ref[...], g_ref[...], sha_ref[...], sca_ref[...]).reshape(tm, d)

    @pl.when(i >= n_a_tiles)
    def _():
        hbuf_ref[...] = _modnorm(hb_ref[...], g_ref[...], shb_ref[...], scb_ref[...]).reshape(tm, d)

    h = hbuf_ref[...]
    _store_slabs(xp_ref, _pack_bf16_pair(h[:, :dh], h[:, dh:]))
    hi = h.astype(BF16)
    lo = (h - hi.astype(F32)).astype(BF16)
    logits = _nt_dot(wh_ref[...], hi) + (_nt_dot(wh_ref[...], lo) + _nt_dot(wl_ref[...], hi))
    scores = jax.nn.sigmoid(logits)
    choice = scores + rb_ref[...]

    sub = lax.broadcasted_iota(I32, (gsz, tm), 0).astype(F32)
    blocks, gs_rows = [], []
    for g in range(N_GROUPS):
        blk = choice[g * gsz:(g + 1) * gsz, :]
        m1 = jnp.max(blk, axis=0, keepdims=True)
        first = jnp.min(jnp.where(blk == m1, sub, float(gsz)), axis=0, keepdims=True)
        m2 = jnp.max(jnp.where(sub == first, -jnp.inf, blk), axis=0, keepdims=True)
        blocks.append(blk)
        gs_rows.append(m1 + m2)
    gs = jnp.concatenate(gs_rows, axis=0)

    def rank_desc(x):
        rows = lax.broadcasted_iota(I32, x.shape, 0)
        rank = jnp.zeros(x.shape, I32)
        for r2 in range(x.shape[0]):
            other = x[r2:r2 + 1, :]
            ahead = jnp.where(other > x, 1, jnp.where((other == x) & (rows > r2), 1, 0))
            rank = rank + ahead
        return rank, rows

    grank, _ = rank_desc(gs)
    masked = jnp.concatenate(
        [jnp.where(grank[g:g + 1, :] < TOPK_GROUPS, blocks[g], NEG_INF) for g in range(N_GROUPS)], axis=0)
    erank, erow = rank_desc(masked)
    sel = erank < TOP_K
    w = jnp.where(sel, scores, 0.0)
    gates = w / jnp.sum(w, axis=0, keepdims=True) * ROUTED_SCALE

    sel_f = jnp.where(sel, 1.0, 0.0)
    incl = jnp.dot(sel_f.astype(BF16), tri_ref[...], preferred_element_type=F32)
    pos_e = run_ref[:, 0:1] + (incl - sel_f)
    run_ref[...] = run_ref[...] + incl[:, tm - 1:tm]
    cnt_ref[...] = run_ref[...]

    erow_f = erow.astype(F32)
    ei_rows, gk_rows, pos_rows = [], [], []
    for r in range(TOP_K):
        hit = erank == r
        ei_rows.append(jnp.sum(jnp.where(hit, erow_f, 0.0), axis=0, keepdims=True))
        gk_rows.append(jnp.sum(jnp.where(hit, gates, 0.0), axis=0, keepdims=True))
        pos_rows.append(jnp.sum(jnp.where(hit, pos_e, 0.0), axis=0, keepdims=True))
    ei_ref[...] = jnp.concatenate(ei_rows, axis=0).astype(I32)
    gk_ref[...] = jnp.concatenate(gk_rows, axis=0)
    pos_ref[...] = jnp.concatenate(pos_rows, axis=0).astype(I32)


def _router(h_a, mod_a, h_b, mod_b, g, wr_hi, wr_lo, rbias):
    ba, sa, d = h_a.shape
    bb, sb, _ = h_b.shape
    rows_a, rows_b = ba * sa, bb * sb
    n_exp = wr_hi.shape[0]
    tm = min(_tile(rows_a, 256), _tile(rows_b, 256))
    assert rows_a % tm == 0 and rows_b % tm == 0 and (d // 2) % LANES == 0
    slab = d // 2 // LANES
    nta, ntb = rows_a // tm, rows_b // tm
    bta, sta = _rows_block(ba, sa, tm)
    btb, stb = _rows_block(bb, sb, tm)
    rmap_a, bmap_a = _row_maps(bta, sa // sta)
    rmap_b, bmap_b = _row_maps(btb, sb // stb)
    ia = lambda i: jnp.minimum(i, nta - 1)
    ib = lambda i: jnp.maximum(i - nta, 0)
    tri = (jnp.arange(tm)[:, None] <= jnp.arange(tm)[None, :]).astype(BF16)
    n_tok = rows_a + rows_b
    cst2 = lambda shape: pl.BlockSpec(shape, lambda i: (0, 0))
    return pl.pallas_call(
        functools.partial(_router_kernel, n_a_tiles=nta),
        grid=(nta + ntb,),
        in_specs=[pl.BlockSpec((bta, sta, d), lambda i: (*rmap_a(ia(i)), 0)),
                  pl.BlockSpec((btb, stb, d), lambda i: (*rmap_b(ib(i)), 0)),
                  pl.BlockSpec((bta, 1, d), lambda i: (bmap_a(ia(i)), 0, 0)),
                  pl.BlockSpec((bta, 1, d), lambda i: (bmap_a(ia(i)), 0, 0)),
                  pl.BlockSpec((btb, 1, d), lambda i: (bmap_b(ib(i)), 0, 0)),
                  pl.BlockSpec((btb, 1, d), lambda i: (bmap_b(ib(i)), 0, 0)),
                  pl.BlockSpec((1, 1, d), lambda i: (0, 0, 0)),
                  cst2((n_exp, d)), cst2((n_exp, d)), cst2((n_exp, 1)), cst2((tm, tm))],
        out_specs=[pl.BlockSpec((tm * slab, LANES), lambda i: (i, 0)),
                   pl.BlockSpec((TOP_K, tm), lambda i: (0, i)),
                   pl.BlockSpec((TOP_K, tm), lambda i: (0, i)),
                   pl.BlockSpec((TOP_K, tm), lambda i: (0, i)),
                   cst2((n_exp, LANES))],
        out_shape=[jax.ShapeDtypeStruct((n_tok * slab, LANES), U32),
                   jax.ShapeDtypeStruct((TOP_K, n_tok), I32),
                   jax.ShapeDtypeStruct((TOP_K, n_tok), F32),
                   jax.ShapeDtypeStruct((TOP_K, n_tok), I32),
                   jax.ShapeDtypeStruct((n_exp, LANES), F32)],
        scratch_shapes=[pltpu.VMEM((tm, d), F32), pltpu.VMEM((n_exp, LANES), F32)],
        compiler_params=_cparams(1),
        name="router",
    )(h_a, h_b, mod_a[0], mod_a[1], mod_b[0], mod_b[1], g.reshape(1, 1, d), wr_hi, wr_lo,
      rbias.reshape(n_exp, 1), tri)


def _dispatch_kernel(lo_ref, hi_ref, slot_ref, x_ref, xs_ref, zero_ref, sem, *, n_tok_tiles, slab):
    i = pl.program_id(0)
    n_dma = slot_ref.shape[2]
    k = TOP_K
    tc = n_dma // k

    def slot_copy(src, slot):
        return pltpu.make_async_copy(src, xs_ref.at[pl.ds(pl.multiple_of(slot * slab, slab), slab)], sem)

    @pl.when(i < n_tok_tiles)
    def _():
        def start(n, carry):
            src = x_ref.at[pl.ds(pl.multiple_of(n * slab, slab), slab)]
            for r in range(k):
                slot_copy(src, slot_ref[0, 0, n * k + r]).start()
            return carry

        def wait(n, carry):
            for r in range(k):
                slot_copy(x_ref.at[pl.ds(0, slab)], 0).wait()
            return carry

        lax.fori_loop(0, tc, start, 0)
        lax.fori_loop(0, tc, wait, 0)

    @pl.when(i >= n_tok_tiles)
    def _():
        e = i - n_tok_tiles
        zero_ref[...] = jnp.zeros(zero_ref.shape, zero_ref.dtype)

        def start(s, carry):
            slot_copy(zero_ref, s).start()
            return carry

        def wait(s, carry):
            slot_copy(zero_ref, 0).wait()
            return carry

        lax.fori_loop(lo_ref[e], hi_ref[e], start, 0)
        lax.fori_loop(lo_ref[e], hi_ref[e], wait, 0)


def _dispatch(xp, slots, pad_lo, pad_hi, n_slots, slab):
    n_tok = xp.shape[0] // slab
    n_exp = pad_lo.shape[0]
    tc = _tile(n_tok, 128)
    ntt = n_tok // tc
    return pl.pallas_call(
        functools.partial(_dispatch_kernel, n_tok_tiles=ntt, slab=slab),
        grid_spec=pltpu.PrefetchScalarGridSpec(
            num_scalar_prefetch=2,
            grid=(ntt + n_exp,),
            in_specs=[pl.BlockSpec((1, 1, tc * TOP_K), lambda i, lo, hi: (jnp.minimum(i, ntt - 1), 0, 0),
                                   memory_space=pltpu.SMEM),
                      pl.BlockSpec((tc * slab, LANES), lambda i, lo, hi: (jnp.minimum(i, ntt - 1), 0))],
            out_specs=pl.BlockSpec(memory_space=pl.ANY),
            scratch_shapes=[pltpu.VMEM((slab, LANES), U32), pltpu.SemaphoreType.DMA(())]),
        out_shape=jax.ShapeDtypeStruct((n_slots * slab, LANES), U32),
        compiler_params=_cparams(1),
        name="dispatch",
    )(pad_lo, pad_hi, slots.reshape(ntt, 1, tc * TOP_K), xp)


def _expert_changed(te_ref, t):
    return (t == 0) | (te_ref[t] != te_ref[jnp.maximum(t - 1, 0)])


def _gateup_kernel(te_ref, nt_ref, x_ref, wg_ref, wu_ref, h_ref, wgb_ref, wub_ref, *, slab):
    t = pl.program_id(1)
    tm = h_ref.shape[0]
    dh = slab * LANES

    @pl.when(t < nt_ref[0])
    def _():
        @pl.when(_expert_changed(te_ref, t))
        def _():
            wgb_ref[...] = wg_ref[0].astype(BF16)
            wub_ref[...] = wu_ref[0].astype(BF16)

        xa, xb = _unpack_bf16_pair(_load_slabs(x_ref, tm, slab))
        xa, xb = xa.astype(BF16), xb.astype(BF16)
        g = (jnp.dot(xa, wgb_ref[:dh, :], preferred_element_type=F32)
             + jnp.dot(xb, wgb_ref[dh:, :], preferred_element_type=F32))
        u = (jnp.dot(xa, wub_ref[:dh, :], preferred_element_type=F32)
             + jnp.dot(xb, wub_ref[dh:, :], preferred_element_type=F32))
        h_ref[...] = (_silu(g) * u).astype(BF16)

    @pl.when(t >= nt_ref[0])
    def _():
        h_ref[...] = jnp.zeros(h_ref.shape, h_ref.dtype)


def _down_kernel(te_ref, nt_ref, h_ref, wd_ref, y_ref, wdb_ref):
    t = pl.program_id(0)
    dh = wd_ref.shape[2] // 2

    @pl.when(t < nt_ref[0])
    def _():
        @pl.when(_expert_changed(te_ref, t))
        def _():
            wdb_ref[...] = wd_ref[0].astype(BF16)

        y = jnp.dot(h_ref[...], wdb_ref[...], preferred_element_type=F32)
        _store_slabs(y_ref, _pack_bf16_pair(y[:, :dh], y[:, dh:]))

    @pl.when(t >= nt_ref[0])
    def _():
        y_ref[...] = jnp.zeros(y_ref.shape, y_ref.dtype)


def _expert_ffn(xs, tile_expert, n_tiles, w_gate, w_up, w_down, tm, slab):
    n_exp, d, f = w_gate.shape
    s_rows = xs.shape[0] // slab
    nt = s_rows // tm
    tf = _tile(f, 512)
    last = lambda t, ntr: jnp.minimum(t, ntr[0] - 1)
    hmid = pl.pallas_call(
        functools.partial(_gateup_kernel, slab=slab),
        grid_spec=pltpu.PrefetchScalarGridSpec(
            num_scalar_prefetch=2,
            grid=(f // tf, nt),
            in_specs=[pl.BlockSpec((tm * slab, LANES), lambda j, t, te, ntr: (last(t, ntr), 0)),
                      pl.BlockSpec((1, d, tf), lambda j, t, te, ntr: (te[t], 0, j)),
                      pl.BlockSpec((1, d, tf), lambda j, t, te, ntr: (te[t], 0, j))],
            out_specs=pl.BlockSpec((tm, tf), lambda j, t, te, ntr: (t, j)),
            scratch_shapes=[pltpu.VMEM((d, tf), BF16), pltpu.VMEM((d, tf), BF16)]),
        out_shape=jax.ShapeDtypeStruct((s_rows, f), BF16),
        compiler_params=_cparams(2),
        name="expert_gateup",
    )(tile_expert, n_tiles, xs, w_gate, w_up)
    return pl.pallas_call(
        _down_kernel,
        grid_spec=pltpu.PrefetchScalarGridSpec(
            num_scalar_prefetch=2,
            grid=(nt,),
            in_specs=[pl.BlockSpec((tm, f), lambda t, te, ntr: (last(t, ntr), 0)),
                      pl.BlockSpec((1, f, d), lambda t, te, ntr: (te[t], 0, 0))],
            out_specs=pl.BlockSpec((tm * slab, LANES), lambda t, te, ntr: (t, 0)),
            scratch_shapes=[pltpu.VMEM((f, d), BF16)]),
        out_shape=jax.ShapeDtypeStruct((s_rows * slab, LANES), U32),
        compiler_params=_cparams(1),
        name="expert_down",
    )(tile_expert, n_tiles, hmid, w_down)


def _shared_kernel(x_ref, wg_ref, wu_ref, wd_ref, o_ref, xa_ref, xb_ref, acc_ref, *, slab):
    j = pl.program_id(1)
    tm, dh = xa_ref.shape

    @pl.when(j == 0)
    def _():
        xa, xb = _unpack_bf16_pair(_load_slabs(x_ref, tm, slab))
        xa_ref[...] = xa.astype(BF16)
        xb_ref[...] = xb.astype(BF16)

    xa, xb = xa_ref[...], xb_ref[...]
    g = (jnp.dot(xa, wg_ref[:dh, :], preferred_element_type=F32)
         + jnp.dot(xb, wg_ref[dh:, :], preferred_element_type=F32))
    u = (jnp.dot(xa, wu_ref[:dh, :], preferred_element_type=F32)
         + jnp.dot(xb, wu_ref[dh:, :], preferred_element_type=F32))
    part = jnp.dot((_silu(g) * u).astype(BF16), wd_ref[...], preferred_element_type=F32)

    @pl.when(j == 0)
    def _():
        acc_ref[...] = part

    @pl.when(j > 0)
    def _():
        acc_ref[...] += part

    @pl.when(j == pl.num_programs(1) - 1)
    def _():
        o_ref[...] = acc_ref[...].astype(BF16)


def _shared_ffn(xp, wg_bf, wu_bf, wd_bf, slab):
    d, f = wg_bf.shape
    rows = xp.shape[0] // slab
    tm = _tile(rows, 512)
    tf = _tile(f, 256)
    return pl.pallas_call(
        functools.partial(_shared_kernel, slab=slab),
        grid=(rows // tm, f // tf),
        in_specs=[pl.BlockSpec((tm * slab, LANES), lambda i, j: (i, 0)),
                  pl.BlockSpec((d, tf), lambda i, j: (0, j)),
                  pl.BlockSpec((d, tf), lambda i, j: (0, j)),
                  pl.BlockSpec((tf, d), lambda i, j: (j, 0))],
        out_specs=pl.BlockSpec((tm, d), lambda i, j: (i, 0)),
        out_shape=jax.ShapeDtypeStruct((rows, d), BF16),
        scratch_shapes=[pltpu.VMEM((tm, d // 2), BF16), pltpu.VMEM((tm, d // 2), BF16), pltpu.VMEM((tm, d), F32)],
        compiler_params=_cparams(2),
        name="shared_ffn",
    )(xp, wg_bf, wu_bf, wd_bf)


def _combine_kernel(slot_ref, slot_next_ref, h_ref, gt_ref, ys_ref, gk_ref, y_hbm, o_ref, buf_ref, sems, *, slab):
    i = pl.program_id(0)
    tc, d = ys_ref.shape
    dh = d // 2
    k = TOP_K

    def slab_copy(slot, half, n, r):
        dst_row = ((half * k + r) * tc + n) * slab
        return pltpu.make_async_copy(y_hbm.at[pl.ds(pl.multiple_of(slot * slab, slab), slab)],
                                     buf_ref.at[pl.ds(pl.multiple_of(dst_row, slab), slab)], sems.at[half])

    def fetch(sref, half):
        def start(n, carry):
            for r in range(k):
                slab_copy(sref[0, 0, n * k + r], half, n, r).start()
            return carry

        lax.fori_loop(0, tc, start, 0)

    def wait(half):
        def body(n, carry):
            for r in range(k):
                slab_copy(0, half, 0, 0).wait()
            return carry

        lax.fori_loop(0, tc, body, 0)

    def weighted_sum(half):
        shp = (h_ref.shape[0], h_ref.shape[1], LANES)
        gks = [gk_ref[:, r:r + 1] for r in range(k)]
        for c in range(slab):
            ca = slice(c * LANES, (c + 1) * LANES)
            cb = slice(dh + c * LANES, dh + (c + 1) * LANES)
            acc_a = ys_ref[:, ca].astype(F32)
            acc_b = ys_ref[:, cb].astype(F32)
            for r in range(k):
                ya, yb = _unpack_bf16_pair(_load_slab_cols(buf_ref, tc, slab, c, row0=(half * k + r) * tc))
                acc_a = acc_a + gks[r] * ya
                acc_b = acc_b + gks[r] * yb
            o_ref[:, :, ca] = h_ref[:, :, ca] + gt_ref[:, :, ca] * acc_a.reshape(shp)
            o_ref[:, :, cb] = h_ref[:, :, cb] + gt_ref[:, :, cb] * acc_b.reshape(shp)

    @pl.when(i == 0)
    def _():
        fetch(slot_ref, 0)

    for half in range(2):
        @pl.when(i % 2 == half)
        def _(half=half):
            @pl.when(i + 1 < pl.num_programs(0))
            def _():
                fetch(slot_next_ref, 1 - half)

            wait(half)
            weighted_sum(half)


def _combine(h1, gate, ysh, ys, slots, gk, row0, slab):
    b, s, d = h1.shape
    rows = b * s
    tc = _tile(rows, 64)
    bt, st = _rows_block(b, s, tc)
    assert row0 % tc == 0
    t0 = row0 // tc
    rmap, bmap = _row_maps(bt, s // st)
    xmap = lambda i: (*rmap(i), 0)
    gmap = lambda i: (bmap(i), 0, 0)
    n_tok = gk.shape[0]
    nsteps = rows // tc
    slots3 = slots.reshape(n_tok // tc, 1, tc * TOP_K)
    return pl.pallas_call(
        functools.partial(_combine_kernel, slab=slab),
        grid=(nsteps,),
        in_specs=[pl.BlockSpec((1, 1, tc * TOP_K), lambda i: (t0 + i, 0, 0), memory_space=pltpu.SMEM),
                  pl.BlockSpec((1, 1, tc * TOP_K), lambda i: (t0 + jnp.minimum(i + 1, nsteps - 1), 0, 0),
                               memory_space=pltpu.SMEM),
                  pl.BlockSpec((bt, st, d), xmap),
                  pl.BlockSpec((bt, 1, d), gmap),
                  pl.BlockSpec((tc, d), lambda i: (t0 + i, 0)),
                  pl.BlockSpec((tc, TOP_K), lambda i: (t0 + i, 0)),
                  pl.BlockSpec(memory_space=pl.ANY)],
        out_specs=pl.BlockSpec((bt, st, d), xmap),
        out_shape=jax.ShapeDtypeStruct((b, s, d), F32),
        scratch_shapes=[pltpu.VMEM((2 * TOP_K * tc * slab, LANES), U32), pltpu.SemaphoreType.DMA((2,))],
        compiler_params=_cparams(1),
        name="combine",
    )(slots3, slots3, h1, gate, ysh, gk, ys)


def _slot_tables(counts, eidx, pos, tm):
    n_exp = counts.shape[0]
    k, n = eidx.shape
    nt_max = (k * n) // tm + n_exp
    tiles = (counts + tm - 1) // tm
    tile_end = jnp.cumsum(tiles)
    off = (tile_end - tiles) * tm
    n_tiles = tile_end[-1:].astype(I32)
    t_ar = jnp.arange(nt_max, dtype=I32)
    te = jnp.minimum(jnp.sum((tile_end[None, :] <= t_ar[:, None]).astype(I32), axis=1), n_exp - 1)
    te_last = jnp.sum(jnp.where(t_ar == n_tiles[0] - 1, te, 0))
    tile_expert = jnp.where(t_ar < n_tiles[0], te, te_last).astype(I32)
    onehot = eidx[None, :, :] == jnp.arange(n_exp, dtype=I32)[:, None, None]
    slot = pos + jnp.sum(jnp.where(onehot, off[:, None, None], 0), axis=0)
    pad_lo = (off + counts).astype(I32)
    pad_hi = jnp.concatenate([off[1:], jnp.full((1,), nt_max * tm, off.dtype)]).astype(I32)
    return slot.T.reshape(n * k).astype(I32), tile_expert, n_tiles, pad_lo, pad_hi, nt_max * tm


def kernel(x_prompt, x_sample, c_prompt, c_sample, cache_k, cache_v, cache_conv, w_ada, b_ada, g_mix, w_in, g_q, g_k, rel_bias, conv_w, conv_b, conv_ln_g, conv_ln_b, beta_attn, beta_conv, w_out, g_ffn, w_router, router_bias, w_gate, w_up, w_down, ws_gate, ws_up, ws_down):
    depth = w_in.shape[0]
    assert depth == 1
    bp, sp, d = x_prompt.shape
    bs, ss, _ = x_sample.shape
    head_dim = g_q.shape[1]
    n_heads = rel_bias.shape[1]
    d_attn = n_heads * head_dim
    d_conv = conv_w.shape[2]
    width = conv_w.shape[1]
    n_exp = w_router.shape[2]
    past_win = cache_k.shape[3]
    l = 0

    n_c = bp + bs
    c_all = jnp.concatenate([c_prompt, c_sample], axis=0)
    c_all = jnp.pad(c_all, ((0, -n_c % 16), (0, 0)))
    mod = _ada(c_all, w_ada[l], b_ada[l])
    mod_p = mod[:bp].reshape(bp, 1, 6 * d)
    mod_s = mod[bp:n_c].reshape(bs, 1, 6 * d)
    part = lambda m, i: m[:, :, i * d:(i + 1) * d]

    w_in_bf = w_in[l].astype(BF16)
    w_out_bf = w_out[l].astype(BF16)
    gh = jnp.concatenate([jnp.tile(g_q[l], n_heads), jnp.tile(g_k[l], n_heads)]).reshape(1, 2 * d_attn)
    tab_p, tab_c, tab_n = _bias_tables(rel_bias[l], LEFT_REACH // 2, ss, past_win)
    wr_t = w_router[l].T
    wr_hi = wr_t.astype(BF16)
    wr_lo = (wr_t - wr_hi.astype(F32)).astype(BF16)

    def mixer(x, m, attn_fn, prefix):
        b, s, _ = x.shape
        z = _inproj(x, part(m, 0), part(m, 1), g_mix[l], w_in_bf, gh, head_dim)
        z3 = z.reshape(b, s, z.shape[1])
        o_attn = attn_fn(z3)
        o_conv, tail = _conv(z3, prefix, conv_w[l], conv_b[l], conv_ln_g[l], conv_ln_b[l], beta_conv[l],
                             d_attn, d_conv)
        h1 = _outproj(o_attn.reshape(b * s, d_attn), o_conv.reshape(b * s, d_conv), beta_attn[l], w_out_bf,
                      x, part(m, 2))
        return h1, z3, tail

    pad_pre = CONV_PAD_ROWS - (width - 1)
    zero_prefix = jnp.zeros((bp, CONV_PAD_ROWS, d_conv), F32)
    samp_prefix = jnp.pad(cache_conv[l], ((0, 0), (pad_pre, 0), (0, 0)))
    h1_p, z3_p, tail_p = mixer(x_prompt, mod_p, lambda z3: _attn_prompt(z3, tab_p, n_heads, head_dim, d_attn),
                               zero_prefix)
    h1_s, z3_s, tail_s = mixer(x_sample, mod_s,
                               lambda z3: _attn_sample(z3, cache_k[l], cache_v[l], tab_c, tab_n, n_heads,
                                                       head_dim, d_attn), samp_prefix)

    slab = d // 2 // LANES
    xp, eidx, gk, pos, cnt = _router(h1_p, (part(mod_p, 3), part(mod_p, 4)), h1_s, (part(mod_s, 3), part(mod_s, 4)),
                                     g_ffn[l], wr_hi, wr_lo, router_bias[l])
    counts = cnt[:, 0].astype(I32)
    slots, tile_expert, n_tiles, pad_lo, pad_hi, n_slots = _slot_tables(counts, eidx, pos, EXPERT_TILE)
    xs = _dispatch(xp, slots, pad_lo, pad_hi, n_slots, slab)
    ys = _expert_ffn(xs, tile_expert, n_tiles, w_gate[l], w_up[l], w_down[l], EXPERT_TILE, slab)
    ysh = _shared_ffn(xp, ws_gate[l].astype(BF16), ws_up[l].astype(BF16), ws_down[l].astype(BF16), slab)
    gk_t = gk.T
    y_p = _combine(h1_p, part(mod_p, 5), ysh, ys, slots, gk_t, 0, slab)
    y_s = _combine(h1_s, part(mod_s, 5), ysh, ys, slots, gk_t, bp * sp, slab)

    def heads(z3, col0, keep):
        b, s, _ = z3.shape
        t = z3[:, s - keep:, col0:col0 + d_attn].astype(F32)
        return t.reshape(b, keep, n_heads, head_dim).transpose(0, 2, 1, 3)[None]

    keep_p = min(LEFT_REACH, sp)
    nw = width - 1
    return (y_p, y_s,
            heads(z3_p, d_attn, keep_p), heads(z3_p, 2 * d_attn, keep_p), tail_p[None, :, CONV_PAD_ROWS - nw:],
            heads(z3_s, d_attn, ss), heads(z3_s, 2 * d_attn, ss), tail_s[None, :, CONV_PAD_ROWS - nw:])
```

```python
import functools

import jax
import jax.numpy as jnp
from jax import lax
from jax.experimental import pallas as pl
from jax.experimental.pallas import tpu as pltpu

F32 = jnp.float32
BF16 = jnp.bfloat16
U32 = jnp.uint32
I32 = jnp.int32

CHUNK = 64
N_LEFT_CHUNKS = 8
LEFT_REACH = N_LEFT_CHUNKS * CHUNK
MAX_REL = 128
N_GROUPS = 8
TOPK_GROUPS = 4
TOP_K = 8
ROUTED_SCALE = 2.5
EPS = 1e-6
NEG_INF = -1e30

LANES = 128
SUBLANES = 8
CONV_PAD_ROWS = 32
V7X_VMEM_LIMIT = 56 * 1024 * 1024
EXPERT_TILE = 512


def _cparams(n_axes):
    return pltpu.CompilerParams(dimension_semantics=("arbitrary",) * n_axes,
                                vmem_limit_bytes=V7X_VMEM_LIMIT)


def _tile(n, pref):
    t = min(n, pref)
    while n % t:
        t //= 2
    return t


def _rows_block(batch, seq, rows):
    st = min(seq, rows)
    bt = rows // st
    assert bt * st == rows and batch % bt == 0 and seq % st == 0
    return bt, st


def _row_maps(bt, spb):
    if bt == 1:
        return (lambda i: (i // spb, i % spb)), (lambda i: i // spb)
    return (lambda i: (i, 0)), (lambda i: i)


def _pack_bf16_pair(a, b):
    au = lax.bitcast_convert_type(a.astype(BF16).astype(F32), U32)
    bu = lax.bitcast_convert_type(b.astype(BF16).astype(F32), U32)
    return au | (bu >> 16)


def _unpack_bf16_pair(p):
    a = lax.bitcast_convert_type(p & jnp.uint32(0xFFFF0000), F32)
    b = lax.bitcast_convert_type(p << 16, F32)
    return a, b


def _store_slabs(ref, val):
    n, w = val.shape
    slab = w // LANES
    for c in range(slab):
        ref[pl.ds(c, n, stride=slab), :] = val[:, c * LANES:(c + 1) * LANES]


def _load_slab_cols(ref, n, slab, c, row0=0):
    return ref[pl.ds(row0 * slab + c, n, stride=slab), :]


def _load_slabs(ref, n, slab):
    return jnp.concatenate([_load_slab_cols(ref, n, slab, c) for c in range(slab)], axis=1)


def _silu(x):
    return x * jax.nn.sigmoid(x)


def _ada_kernel(c_ref, w_ref, b_ref, o_ref):
    a = _silu(c_ref[...]).astype(BF16)
    o_ref[...] = jnp.dot(a, w_ref[...].astype(BF16), preferred_element_type=F32) + b_ref[...]


def _ada(c, w, b):
    m, d = c.shape
    n = w.shape[1]
    tn = _tile(n, 512)
    return pl.pallas_call(
        _ada_kernel,
        grid=(n // tn,),
        in_specs=[pl.BlockSpec((m, d), lambda j: (0, 0)),
                  pl.BlockSpec((d, tn), lambda j: (0, j)),
                  pl.BlockSpec((1, tn), lambda j: (0, j))],
        out_specs=pl.BlockSpec((m, tn), lambda j: (0, j)),
        out_shape=jax.ShapeDtypeStruct((m, n), F32),
        compiler_params=_cparams(1),
        name="ada",
    )(c, w, b.reshape(1, n))


def _modnorm(x, g, shift, scale):
    ms = jnp.mean(x * x, axis=-1, keepdims=True)
    y = x * lax.rsqrt(ms + EPS) * g
    return y * (1.0 + scale) + shift


def _inproj_kernel(x_ref, sh_ref, sc_ref, g_ref, w_ref, gh_ref, o_ref, hn_ref, *, n_norm_tiles, head_dim):
    j = pl.program_id(1)
    tm, d = hn_ref.shape
    tn = o_ref.shape[1]

    @pl.when(j == 0)
    def _():
        h = _modnorm(x_ref[...], g_ref[...], sh_ref[...], sc_ref[...])
        hn_ref[...] = h.reshape(tm, d).astype(BF16)

    acc = jnp.dot(hn_ref[...], w_ref[...], preferred_element_type=F32)

    @pl.when(j < n_norm_tiles)
    def _():
        for hh in range(tn // head_dim):
            sl = slice(hh * head_dim, (hh + 1) * head_dim)
            a = acc[:, sl]
            ms = jnp.mean(a * a, axis=-1, keepdims=True)
            o_ref[:, sl] = (a * lax.rsqrt(ms + EPS) * gh_ref[:, sl]).astype(BF16)

    @pl.when(j >= n_norm_tiles)
    def _():
        o_ref[...] = acc.astype(BF16)


def _inproj(x, shift, scale, g, w_bf, gh, head_dim):
    b, s, d = x.shape
    n = w_bf.shape[1]
    rows = b * s
    tm = _tile(rows, 512)
    tn = _tile(n, 1024)
    bt, st = _rows_block(b, s, tm)
    n_norm = gh.shape[1] // tn
    assert n_norm * tn == gh.shape[1] and tn % head_dim == 0
    rmap, bmap = _row_maps(bt, s // st)
    xmap = lambda i, j: (*rmap(i), 0)
    mmap = lambda i, j: (bmap(i), 0, 0)
    return pl.pallas_call(
        functools.partial(_inproj_kernel, n_norm_tiles=n_norm, head_dim=head_dim),
        grid=(rows // tm, n // tn),
        in_specs=[pl.BlockSpec((bt, st, d), xmap),
                  pl.BlockSpec((bt, 1, d), mmap),
                  pl.BlockSpec((bt, 1, d), mmap),
                  pl.BlockSpec((1, 1, d), lambda i, j: (0, 0, 0)),
                  pl.BlockSpec((d, tn), lambda i, j: (0, j)),
                  pl.BlockSpec((1, tn), lambda i, j: (0, jnp.minimum(j, n_norm - 1)))],
        out_specs=pl.BlockSpec((tm, tn), lambda i, j: (i, j)),
        out_shape=jax.ShapeDtypeStruct((rows, n), BF16),
        scratch_shapes=[pltpu.VMEM((tm, d), BF16)],
        compiler_params=_cparams(2),
        name="inproj",
    )(x, shift, scale, g.reshape(1, 1, d), w_bf, gh)


def _softmax_pv(s_list, v_list):
    m = functools.reduce(jnp.maximum, [jnp.max(s, axis=-1, keepdims=True) for s in s_list])
    p_list = [jnp.exp(s - m) for s in s_list]
    l = functools.reduce(jnp.add, [jnp.sum(p, axis=-1, keepdims=True) for p in p_list])
    o = functools.reduce(jnp.add, [jnp.dot(p.astype(BF16), v, preferred_element_type=F32)
                                   for p, v in zip(p_list, v_list)])
    return o / l


def _nt_dot(a, b):
    return lax.dot_general(a, b, (((1,), (1,)), ((), ())), preferred_element_type=F32)


def _attn_prompt_kernel(q_ref, k0_ref, k1_ref, k2_ref, v0_ref, v1_ref, v2_ref, t_ref, o_ref, *, scale, head_dim):
    i = pl.program_id(2)
    qb = q_ref.shape[1]
    for g in range(q_ref.shape[2] // head_dim):
        sl = slice(g * head_dim, (g + 1) * head_dim)
        q = q_ref[0, :, sl]
        k = jnp.concatenate([k0_ref[0, :, sl], k1_ref[0, :, sl], k2_ref[0, :, sl]], axis=0)
        v = jnp.concatenate([v0_ref[0, :, sl], v1_ref[0, :, sl], v2_ref[0, :, sl]], axis=0)
        s = _nt_dot(q, k) * scale + t_ref[g]
        col = lax.broadcasted_iota(I32, s.shape, 1)
        s = jnp.where(col < (2 - i) * qb, NEG_INF, s)
        o_ref[0, :, sl] = _softmax_pv([s], [v]).astype(BF16)


def _attn_prompt(z3, table, n_heads, head_dim, d_attn):
    b, s, _ = z3.shape
    qb = LEFT_REACH // 2
    hg = _tile(n_heads, 4)
    wg = hg * head_dim
    assert s % qb == 0 and table.shape == (n_heads, qb, 3 * qb)
    ck, cv = d_attn // wg, 2 * d_attn // wg

    def kv_spec(off, back):
        return pl.BlockSpec((1, qb, wg), lambda bb, h, i: (bb, jnp.maximum(i - back, 0), off + h))

    return pl.pallas_call(
        functools.partial(_attn_prompt_kernel, scale=head_dim ** -0.5, head_dim=head_dim),
        grid=(b, n_heads // hg, s // qb),
        in_specs=[pl.BlockSpec((1, qb, wg), lambda bb, h, i: (bb, i, h)),
                  kv_spec(ck, 2), kv_spec(ck, 1), kv_spec(ck, 0),
                  kv_spec(cv, 2), kv_spec(cv, 1), kv_spec(cv, 0),
                  pl.BlockSpec((hg, qb, 3 * qb), lambda bb, h, i: (h, 0, 0))],
        out_specs=pl.BlockSpec((1, qb, wg), lambda bb, h, i: (bb, i, h)),
        out_shape=jax.ShapeDtypeStruct((b, s, d_attn), BF16),
        compiler_params=_cparams(3),
        name="attn_prompt",
    )(z3, z3, z3, z3, z3, z3, z3, table)


def _attn_sample_kernel(q_ref, kn_ref, vn_ref, ck_ref, cv_ref, tc_ref, tn_ref, o_ref, *, scale, head_dim):
    for g in range(q_ref.shape[2] // head_dim):
        sl = slice(g * head_dim, (g + 1) * head_dim)
        q = q_ref[0, :, sl]
        ck = ck_ref[0, g].astype(BF16)
        cv = cv_ref[0, g].astype(BF16)
        s_c = _nt_dot(q, ck) * scale + tc_ref[g]
        s_n = _nt_dot(q, kn_ref[0, :, sl]) * scale + tn_ref[g]
        o_ref[0, :, sl] = _softmax_pv([s_c, s_n], [cv, vn_ref[0, :, sl]]).astype(BF16)


def _attn_sample(z3, cache_k, cache_v, tab_c, tab_n, n_heads, head_dim, d_attn):
    b, t, _ = z3.shape
    w = cache_k.shape[2]
    hg = _tile(n_heads, 8)
    wg = hg * head_dim
    ck, cv = d_attn // wg, 2 * d_attn // wg
    return pl.pallas_call(
        functools.partial(_attn_sample_kernel, scale=head_dim ** -0.5, head_dim=head_dim),
        grid=(b, n_heads // hg),
        in_specs=[pl.BlockSpec((1, t, wg), lambda bb, h: (bb, 0, h)),
                  pl.BlockSpec((1, t, wg), lambda bb, h: (bb, 0, ck + h)),
                  pl.BlockSpec((1, t, wg), lambda bb, h: (bb, 0, cv + h)),
                  pl.BlockSpec((1, hg, w, head_dim), lambda bb, h: (bb, h, 0, 0)),
                  pl.BlockSpec((1, hg, w, head_dim), lambda bb, h: (bb, h, 0, 0)),
                  pl.BlockSpec((hg, t, w), lambda bb, h: (h, 0, 0)),
                  pl.BlockSpec((hg, t, t), lambda bb, h: (h, 0, 0))],
        out_specs=pl.BlockSpec((1, t, wg), lambda bb, h: (bb, 0, h)),
        out_shape=jax.ShapeDtypeStruct((b, t, d_attn), BF16),
        compiler_params=_cparams(2),
        name="attn_sample",
    )(z3, z3, z3, cache_k, cache_v, tab_c, tab_n)


def _rel_bias_tile(rel_bias, n_rows, n_cols, delta0):
    col0 = n_rows - 1
    length = col0 + n_cols + 1
    idx = jnp.clip(jnp.arange(length) - col0 + delta0, -MAX_REL, MAX_REL) + MAX_REL
    ext = rel_bias[:, idx]
    h = ext.shape[0]
    skew = jnp.tile(ext, (1, n_rows))[:, :n_rows * (length - 1)].reshape(h, n_rows, length - 1)
    return skew[:, :, col0:col0 + n_cols].astype(F32)


def _bias_tables(rel_bias, qb, t_new, past_win):
    r = jnp.arange(qb)[:, None]
    j = jnp.arange(3 * qb)[None, :]
    dchunk = j // CHUNK - r // CHUNK
    ok = (dchunk >= 0) & (dchunk <= N_LEFT_CHUNKS)
    tab_p = jnp.where(ok[None], _rel_bias_tile(rel_bias, qb, 3 * qb, -2 * qb), NEG_INF)
    tab_c = _rel_bias_tile(rel_bias, t_new, past_win, -past_win)
    tab_n = _rel_bias_tile(rel_bias, t_new, t_new, 0)
    return tab_p, tab_c, tab_n


def _conv_kernel(a_ref, g_ref, pre_ref, cw_ref, cb_ref, lg_ref, lb_ref, beta_ref, o_ref, tail_ref, buf_ref, y_ref,
                 *, width):
    t = pl.program_id(1)
    ts = a_ref.shape[1]
    rc, c_all = y_ref.shape
    pad = CONV_PAD_ROWS

    @pl.when(t == 0)
    def _():
        buf_ref[0:pad, :] = pre_ref[0]

    @pl.when(t > 0)
    def _():
        buf_ref[0:pad, :] = buf_ref[ts:ts + pad, :]

    buf_ref[pad:pad + ts, :] = a_ref[0].astype(F32) * jax.nn.sigmoid(g_ref[0].astype(F32))
    tail_ref[0] = buf_ref[ts:ts + pad, :]
    first = pad - (width - 1)
    taps_by_shift = [[w for w in range(width) if (first + w) % SUBLANES == b] for b in range(SUBLANES)]

    def chunk(ci, carry):
        base = pl.multiple_of(ci * rc, rc)

        def colblock(cb, carry2):
            col = pl.multiple_of(cb * LANES, LANES)
            win = buf_ref[pl.ds(base, rc + pad), pl.ds(col, LANES)]
            acc = None
            for b, taps in enumerate(taps_by_shift):
                if not taps:
                    continue
                sh = win if b == 0 else pltpu.roll(win, rc + pad - b, axis=0)
                for w in taps:
                    a8 = (first + w) // SUBLANES * SUBLANES
                    term = sh[a8:a8 + rc, :] * cw_ref[pl.ds(w, 1), pl.ds(col, LANES)]
                    acc = term if acc is None else acc + term
            y_ref[:, pl.ds(col, LANES)] = acc
            return carry2

        lax.fori_loop(0, c_all // LANES, colblock, 0)
        y = y_ref[...] + cb_ref[...]
        mu = jnp.mean(y, axis=-1, keepdims=True)
        yc = y - mu
        var = jnp.mean(yc * yc, axis=-1, keepdims=True)
        s = _silu(yc * lax.rsqrt(var + EPS) * lg_ref[...] + lb_ref[...])
        ms = jnp.mean(s * s, axis=-1, keepdims=True)
        o_ref[0, pl.ds(base, rc), :] = (s * lax.rsqrt(ms + EPS) * beta_ref[...]).astype(BF16)
        return carry

    lax.fori_loop(0, ts // rc, chunk, 0)


def _conv(z3, prefix, conv_w, conv_b, ln_g, ln_b, beta, d_attn, d_conv):
    b, t, _ = z3.shape
    width = conv_w.shape[0]
    ts = _tile(t, 256)
    rc = _tile(ts, 64)
    assert 3 * d_attn % d_conv == 0 and rc % 16 == 0 and ts >= CONV_PAD_ROWS and width - 1 <= CONV_PAD_ROWS
    ca = 3 * d_attn // d_conv
    vec = lambda x: x.reshape(1, d_conv)
    cst = lambda shape: pl.BlockSpec(shape, lambda bb, tt: (0,) * len(shape))
    return pl.pallas_call(
        functools.partial(_conv_kernel, width=width),
        grid=(b, t // ts),
        in_specs=[pl.BlockSpec((1, ts, d_conv), lambda bb, tt: (bb, tt, ca)),
                  pl.BlockSpec((1, ts, d_conv), lambda bb, tt: (bb, tt, ca + 1)),
                  pl.BlockSpec((1, CONV_PAD_ROWS, d_conv), lambda bb, tt: (bb, 0, 0)),
                  cst((width, d_conv)), cst((1, d_conv)), cst((1, d_conv)), cst((1, d_conv)), cst((1, d_conv))],
        out_specs=[pl.BlockSpec((1, ts, d_conv), lambda bb, tt: (bb, tt, 0)),
                   pl.BlockSpec((1, CONV_PAD_ROWS, d_conv), lambda bb, tt: (bb, 0, 0))],
        out_shape=[jax.ShapeDtypeStruct((b, t, d_conv), BF16),
                   jax.ShapeDtypeStruct((b, CONV_PAD_ROWS, d_conv), F32)],
        scratch_shapes=[pltpu.VMEM((CONV_PAD_ROWS + ts, d_conv), F32), pltpu.VMEM((rc, d_conv), F32)],
        compiler_params=_cparams(2),
        name="conv",
    )(z3, z3, prefix, conv_w, vec(conv_b), vec(ln_g), vec(ln_b), vec(beta))


def _outproj_kernel(oa_ref, oc_ref, ba_ref, w_ref, x_ref, gt_ref, o_ref, mx_ref):
    j = pl.program_id(1)
    da = oa_ref.shape[1]

    @pl.when(j == 0)
    def _():
        oa = oa_ref[...].astype(F32)
        ms = jnp.mean(oa * oa, axis=-1, keepdims=True)
        mx_ref[:, :da] = (oa * lax.rsqrt(ms + EPS) * ba_ref[...]).astype(BF16)
        mx_ref[:, da:] = oc_ref[...]

    acc = jnp.dot(mx_ref[...], w_ref[...], preferred_element_type=F32)
    o_ref[...] = x_ref[...] + gt_ref[...] * acc.reshape(x_ref.shape)


def _outproj(o_attn, o_conv, beta_attn, w_bf, x, gate):
    b, s, d = x.shape
    rows = b * s
    da, dc = o_attn.shape[1], o_conv.shape[1]
    tm = _tile(rows, 512)
    tn = _tile(d, 1024)
    bt, st = _rows_block(b, s, tm)
    rmap, bmap = _row_maps(bt, s // st)
    xmap = lambda i, j: (*rmap(i), j)
    gmap = lambda i, j: (bmap(i), 0, j)
    return pl.pallas_call(
        _outproj_kernel,
        grid=(rows // tm, d // tn),
        in_specs=[pl.BlockSpec((tm, da), lambda i, j: (i, 0)),
                  pl.BlockSpec((tm, dc), lambda i, j: (i, 0)),
                  pl.BlockSpec((1, da), lambda i, j: (0, 0)),
                  pl.BlockSpec((da + dc, tn), lambda i, j: (0, j)),
                  pl.BlockSpec((bt, st, tn), xmap),
                  pl.BlockSpec((bt, 1, tn), gmap)],
        out_specs=pl.BlockSpec((bt, st, tn), xmap),
        out_shape=jax.ShapeDtypeStruct((b, s, d), F32),
        scratch_shapes=[pltpu.VMEM((tm, da + dc), BF16)],
        compiler_params=_cparams(2),
        name="outproj",
    )(o_attn, o_conv, beta_attn.reshape(1, da), w_bf, x, gate)


def _router_kernel(ha_ref, hb_ref, sha_ref, sca_ref, shb_ref, scb_ref, g_ref, wh_ref, wl_ref, rb_ref, tri_ref,
                   xp_ref, ei_ref, gk_ref, pos_ref, cnt_ref, hbuf_ref, run_ref, *, n_a_tiles):
    i = pl.program_id(0)
    tm, d = hbuf_ref.shape
    dh = d // 2
    n_exp = wh_ref.shape[0]
    gsz = n_exp // N_GROUPS

    @pl.when(i == 0)
    def _():
        run_ref[...] = jnp.zeros(run_ref.shape, F32)

    @pl.when(i < n_a_tiles)
    def _():
        hbuf_ref[...] = _modnorm(ha_ref[...], g_ref[...], sha_ref[...], sca_ref[...]).reshape(tm, d)

    @pl.when(i >= n_a_tiles)
    def _():
        hbuf_ref[...] = _modnorm(hb_ref[...], g_ref[...], shb_ref[...], scb_ref[...]).reshape(tm, d)

    h = hbuf_ref[...]
    _store_slabs(xp_ref, _pack_bf16_pair(h[:, :dh], h[:, dh:]))
    hi = h.astype(BF16)
    lo = (h - hi.astype(F32)).astype(BF16)
    logits = _nt_dot(wh_ref[...], hi) + (_nt_dot(wh_ref[...], lo) + _nt_dot(wl_ref[...], hi))
    scores = jax.nn.sigmoid(logits)
    choice = scores + rb_ref[...]

    sub = lax.broadcasted_iota(I32, (gsz, tm), 0).astype(F32)
    blocks, gs_rows = [], []
    for g in range(N_GROUPS):
        blk = choice[g * gsz:(g + 1) * gsz, :]
        m1 = jnp.max(blk, axis=0, keepdims=True)
        first = jnp.min(jnp.where(blk == m1, sub, float(gsz)), axis=0, keepdims=True)
        m2 = jnp.max(jnp.where(sub == first, -jnp.inf, blk), axis=0, keepdims=True)
        blocks.append(blk)
        gs_rows.append(m1 + m2)
    gs = jnp.concatenate(gs_rows, axis=0)

    def rank_desc(x):
        rows = lax.broadcasted_iota(I32, x.shape, 0)
        rank = jnp.zeros(x.shape, I32)
        for r2 in range(x.shape[0]):
            other = x[r2:r2 + 1, :]
            ahead = jnp.where(other > x, 1, jnp.where((other == x) & (rows > r2), 1, 0))
            rank = rank + ahead
        return rank, rows

    grank, _ = rank_desc(gs)
    masked = jnp.concatenate(
        [jnp.where(grank[g:g + 1, :] < TOPK_GROUPS, blocks[g], NEG_INF) for g in range(N_GROUPS)], axis=0)
    erank, erow = rank_desc(masked)
    sel = erank < TOP_K
    w = jnp.where(sel, scores, 0.0)
    gates = w / jnp.sum(w, axis=0, keepdims=True) * ROUTED_SCALE

    sel_f = jnp.where(sel, 1.0, 0.0)
    incl = jnp.dot(sel_f.astype(BF16), tri_ref[...], preferred_element_type=F32)
    pos_e = run_ref[:, 0:1] + (incl - sel_f)
    run_ref[...] = run_ref[...] + incl[:, tm - 1:tm]
    cnt_ref[...] = run_ref[...]

    erow_f = erow.astype(F32)
    ei_rows, gk_rows, pos_rows = [], [], []
    for r in range(TOP_K):
        hit = erank == r
        ei_rows.append(jnp.sum(jnp.where(hit, erow_f, 0.0), axis=0, keepdims=True))
        gk_rows.append(jnp.sum(jnp.where(hit, gates, 0.0), axis=0, keepdims=True))
        pos_rows.append(jnp.sum(jnp.where(hit, pos_e, 0.0), axis=0, keepdims=True))
    ei_ref[...] = jnp.concatenate(ei_rows, axis=0).astype(I32)
    gk_ref[...] = jnp.concatenate(gk_rows, axis=0)
    pos_ref[...] = jnp.concatenate(pos_rows, axis=0).astype(I32)


def _router(h_a, mod_a, h_b, mod_b, g, wr_hi, wr_lo, rbias):
    ba, sa, d = h_a.shape
    bb, sb, _ = h_b.shape
    rows_a, rows_b = ba * sa, bb * sb
    n_exp = wr_hi.shape[0]
    tm = min(_tile(rows_a, 256), _tile(rows_b, 256))
    assert rows_a % tm == 0 and rows_b % tm == 0 and (d // 2) % LANES == 0
    slab = d // 2 // LANES
    nta, ntb = rows_a // tm, rows_b // tm
    bta, sta = _rows_block(ba, sa, tm)
    btb, stb = _rows_block(bb, sb, tm)
    rmap_a, bmap_a = _row_maps(bta, sa // sta)
    rmap_b, bmap_b = _row_maps(btb, sb // stb)
    ia = lambda i: jnp.minimum(i, nta - 1)
    ib = lambda i: jnp.maximum(i - nta, 0)
    tri = (jnp.arange(tm)[:, None] <= jnp.arange(tm)[None, :]).astype(BF16)
    n_tok = rows_a + rows_b
    cst2 = lambda shape: pl.BlockSpec(shape, lambda i: (0, 0))
    return pl.pallas_call(
        functools.partial(_router_kernel, n_a_tiles=nta),
        grid=(nta + ntb,),
        in_specs=[pl.BlockSpec((bta, sta, d), lambda i: (*rmap_a(ia(i)), 0)),
                  pl.BlockSpec((btb, stb, d), lambda i: (*rmap_b(ib(i)), 0)),
                  pl.BlockSpec((bta, 1, d), lambda i: (bmap_a(ia(i)), 0, 0)),
                  pl.BlockSpec((bta, 1, d), lambda i: (bmap_a(ia(i)), 0, 0)),
                  pl.BlockSpec((btb, 1, d), lambda i: (bmap_b(ib(i)), 0, 0)),
                  pl.BlockSpec((btb, 1, d), lambda i: (bmap_b(ib(i)), 0, 0)),
                  pl.BlockSpec((1, 1, d), lambda i: (0, 0, 0)),
                  cst2((n_exp, d)), cst2((n_exp, d)), cst2((n_exp, 1)), cst2((tm, tm))],
        out_specs=[pl.BlockSpec((tm * slab, LANES), lambda i: (i, 0)),
                   pl.BlockSpec((TOP_K, tm), lambda i: (0, i)),
                   pl.BlockSpec((TOP_K, tm), lambda i: (0, i)),
                   pl.BlockSpec((TOP_K, tm), lambda i: (0, i)),
                   cst2((n_exp, LANES))],
        out_shape=[jax.ShapeDtypeStruct((n_tok * slab, LANES), U32),
                   jax.ShapeDtypeStruct((TOP_K, n_tok), I32),
                   jax.ShapeDtypeStruct((TOP_K, n_tok), F32),
                   jax.ShapeDtypeStruct((TOP_K, n_tok), I32),
                   jax.ShapeDtypeStruct((n_exp, LANES), F32)],
        scratch_shapes=[pltpu.VMEM((tm, d), F32), pltpu.VMEM((n_exp, LANES), F32)],
        compiler_params=_cparams(1),
        name="router",
    )(h_a, h_b, mod_a[0], mod_a[1], mod_b[0], mod_b[1], g.reshape(1, 1, d), wr_hi, wr_lo,
      rbias.reshape(n_exp, 1), tri)


def _shared_dispatch_kernel(lo_ref, hi_ref, slot_ref, x_ref, wg_ref, wu_ref, wd_ref, o_ref, xs_ref,
                            xa_ref, xb_ref, acc_ref, zero_ref, sem, *, n_tok_tiles, slab):
    i = pl.program_id(0)
    j = pl.program_id(1)
    last_j = pl.num_programs(1) - 1
    tm, dh = xa_ref.shape
    k = TOP_K

    def slot_copy(src, slot):
        return pltpu.make_async_copy(src, xs_ref.at[pl.ds(pl.multiple_of(slot * slab, slab), slab)], sem)

    @pl.when(i < n_tok_tiles)
    def _():
        @pl.when(j == 0)
        def _():
            def start(n, carry):
                src = x_ref.at[pl.ds(pl.multiple_of(n * slab, slab), slab)]
                for r in range(k):
                    slot_copy(src, slot_ref[0, 0, n * k + r]).start()
                return carry

            lax.fori_loop(0, tm, start, 0)
            xa, xb = _unpack_bf16_pair(_load_slabs(x_ref, tm, slab))
            xa_ref[...] = xa.astype(BF16)
            xb_ref[...] = xb.astype(BF16)

        xa, xb = xa_ref[...], xb_ref[...]
        g = (jnp.dot(xa, wg_ref[:dh, :], preferred_element_type=F32)
             + jnp.dot(xb, wg_ref[dh:, :], preferred_element_type=F32))
        u = (jnp.dot(xa, wu_ref[:dh, :], preferred_element_type=F32)
             + jnp.dot(xb, wu_ref[dh:, :], preferred_element_type=F32))
        part = jnp.dot((_silu(g) * u).astype(BF16), wd_ref[...], preferred_element_type=F32)

        @pl.when(j == 0)
        def _():
            acc_ref[...] = part

        @pl.when(j > 0)
        def _():
            acc_ref[...] += part

        @pl.when(j == last_j)
        def _():
            o_ref[...] = acc_ref[...].astype(BF16)

            def wait(n, carry):
                for r in range(k):
                    slot_copy(x_ref.at[pl.ds(0, slab)], 0).wait()
                return carry

            lax.fori_loop(0, tm, wait, 0)

    @pl.when((i >= n_tok_tiles) & (j == 0))
    def _():
        e = i - n_tok_tiles
        zero_ref[...] = jnp.zeros(zero_ref.shape, zero_ref.dtype)

        def start(s, carry):
            slot_copy(zero_ref, s).start()
            return carry

        def wait(s, carry):
            slot_copy(zero_ref, 0).wait()
            return carry

        lax.fori_loop(lo_ref[e], hi_ref[e], start, 0)
        lax.fori_loop(lo_ref[e], hi_ref[e], wait, 0)


def _shared_dispatch(xp, slots, pad_lo, pad_hi, n_slots, slab, wg_bf, wu_bf, wd_bf):
    d, f = wg_bf.shape
    n_tok = xp.shape[0] // slab
    n_exp = pad_lo.shape[0]
    tm = _tile(n_tok, 512)
    tf = _tile(f, 256)
    ntt, nf = n_tok // tm, f // tf
    row = lambda i: jnp.minimum(i, ntt - 1)
    col = lambda i, j: jnp.where(i < ntt, j, nf - 1)
    return pl.pallas_call(
        functools.partial(_shared_dispatch_kernel, n_tok_tiles=ntt, slab=slab),
        grid_spec=pltpu.PrefetchScalarGridSpec(
            num_scalar_prefetch=2,
            grid=(ntt + n_exp, nf),
            in_specs=[pl.BlockSpec((1, 1, tm * TOP_K), lambda i, j, lo, hi: (row(i), 0, 0),
                                   memory_space=pltpu.SMEM),
                      pl.BlockSpec((tm * slab, LANES), lambda i, j, lo, hi: (row(i), 0)),
                      pl.BlockSpec((d, tf), lambda i, j, lo, hi: (0, col(i, j))),
                      pl.BlockSpec((d, tf), lambda i, j, lo, hi: (0, col(i, j))),
                      pl.BlockSpec((tf, d), lambda i, j, lo, hi: (col(i, j), 0))],
            out_specs=[pl.BlockSpec((tm, d), lambda i, j, lo, hi: (row(i), 0)),
                       pl.BlockSpec(memory_space=pl.ANY)],
            scratch_shapes=[pltpu.VMEM((tm, d // 2), BF16), pltpu.VMEM((tm, d // 2), BF16),
                            pltpu.VMEM((tm, d), F32), pltpu.VMEM((slab, LANES), U32),
                            pltpu.SemaphoreType.DMA(())]),
        out_shape=[jax.ShapeDtypeStruct((n_tok, d), BF16),
                   jax.ShapeDtypeStruct((n_slots * slab, LANES), U32)],
        compiler_params=_cparams(2),
        name="shared_dispatch",
    )(pad_lo, pad_hi, slots.reshape(ntt, 1, tm * TOP_K), xp, wg_bf, wu_bf, wd_bf)


def _expert_changed(te_ref, t):
    return (t == 0) | (te_ref[t] != te_ref[jnp.maximum(t - 1, 0)])


def _gateup_kernel(te_ref, nt_ref, x_ref, wg_ref, wu_ref, h_ref, wgb_ref, wub_ref, *, slab):
    t = pl.program_id(1)
    tm = h_ref.shape[0]
    dh = slab * LANES

    @pl.when(t < nt_ref[0])
    def _():
        @pl.when(_expert_changed(te_ref, t))
        def _():
            wgb_ref[...] = wg_ref[0].astype(BF16)
            wub_ref[...] = wu_ref[0].astype(BF16)

        xa, xb = _unpack_bf16_pair(_load_slabs(x_ref, tm, slab))
        xa, xb = xa.astype(BF16), xb.astype(BF16)
        g = (jnp.dot(xa, wgb_ref[:dh, :], preferred_element_type=F32)
             + jnp.dot(xb, wgb_ref[dh:, :], preferred_element_type=F32))
        u = (jnp.dot(xa, wub_ref[:dh, :], preferred_element_type=F32)
             + jnp.dot(xb, wub_ref[dh:, :], preferred_element_type=F32))
        h_ref[...] = (_silu(g) * u).astype(BF16)

    @pl.when(t >= nt_ref[0])
    def _():
        h_ref[...] = jnp.zeros(h_ref.shape, h_ref.dtype)


def _down_kernel(te_ref, nt_ref, h_ref, wd_ref, y_ref, wdb_ref):
    t = pl.program_id(0)
    dh = wd_ref.shape[2] // 2

    @pl.when(t < nt_ref[0])
    def _():
        @pl.when(_expert_changed(te_ref, t))
        def _():
            wdb_ref[...] = wd_ref[0].astype(BF16)

        y = jnp.dot(h_ref[...], wdb_ref[...], preferred_element_type=F32)
        _store_slabs(y_ref, _pack_bf16_pair(y[:, :dh], y[:, dh:]))

    @pl.when(t >= nt_ref[0])
    def _():
        y_ref[...] = jnp.zeros(y_ref.shape, y_ref.dtype)


def _expert_ffn(xs, tile_expert, n_tiles, w_gate, w_up, w_down, tm, slab):
    n_exp, d, f = w_gate.shape
    s_rows = xs.shape[0] // slab
    nt = s_rows // tm
    tf = _tile(f, 512)
    last = lambda t, ntr: jnp.minimum(t, ntr[0] - 1)
    hmid = pl.pallas_call(
        functools.partial(_gateup_kernel, slab=slab),
        grid_spec=pltpu.PrefetchScalarGridSpec(
            num_scalar_prefetch=2,
            grid=(f // tf, nt),
            in_specs=[pl.BlockSpec((tm * slab, LANES), lambda j, t, te, ntr: (last(t, ntr), 0)),
                      pl.BlockSpec((1, d, tf), lambda j, t, te, ntr: (te[t], 0, j)),
                      pl.BlockSpec((1, d, tf), lambda j, t, te, ntr: (te[t], 0, j))],
            out_specs=pl.BlockSpec((tm, tf), lambda j, t, te, ntr: (t, j)),
            scratch_shapes=[pltpu.VMEM((d, tf), BF16), pltpu.VMEM((d, tf), BF16)]),
        out_shape=jax.ShapeDtypeStruct((s_rows, f), BF16),
        compiler_params=_cparams(2),
        name="expert_gateup",
    )(tile_expert, n_tiles, xs, w_gate, w_up)
    return pl.pallas_call(
        _down_kernel,
        grid_spec=pltpu.PrefetchScalarGridSpec(
            num_scalar_prefetch=2,
            grid=(nt,),
            in_specs=[pl.BlockSpec((tm, f), lambda t, te, ntr: (last(t, ntr), 0)),
                      pl.BlockSpec((1, f, d), lambda t, te, ntr: (te[t], 0, 0))],
            out_specs=pl.BlockSpec((tm * slab, LANES), lambda t, te, ntr: (t, 0)),
            scratch_shapes=[pltpu.VMEM((f, d), BF16)]),
        out_shape=jax.ShapeDtypeStruct((s_rows * slab, LANES), U32),
        compiler_params=_cparams(1),
        name="expert_down",
    )(tile_expert, n_tiles, hmid, w_down)


def _combine_kernel(slot_ref, slot_next_ref, h_ref, gt_ref, ys_ref, gk_ref, y_hbm, o_ref, buf_ref, sems, *, slab):
    i = pl.program_id(0)
    tc, d = ys_ref.shape
    dh = d // 2
    k = TOP_K

    def slab_copy(slot, half, n, r):
        dst_row = ((half * k + r) * tc + n) * slab
        return pltpu.make_async_copy(y_hbm.at[pl.ds(pl.multiple_of(slot * slab, slab), slab)],
                                     buf_ref.at[pl.ds(pl.multiple_of(dst_row, slab), slab)], sems.at[half])

    def fetch(sref, half):
        def start(n, carry):
            for r in range(k):
                slab_copy(sref[0, 0, n * k + r], half, n, r).start()
            return carry

        lax.fori_loop(0, tc, start, 0)

    def wait(half):
        def body(n, carry):
            for r in range(k):
                slab_copy(0, half, 0, 0).wait()
            return carry

        lax.fori_loop(0, tc, body, 0)

    def weighted_sum(half):
        shp = (h_ref.shape[0], h_ref.shape[1], LANES)
        gks = [gk_ref[:, r:r + 1] for r in range(k)]
        for c in range(slab):
            ca = slice(c * LANES, (c + 1) * LANES)
            cb = slice(dh + c * LANES, dh + (c + 1) * LANES)
            acc_a = ys_ref[:, ca].astype(F32)
            acc_b = ys_ref[:, cb].astype(F32)
            for r in range(k):
                ya, yb = _unpack_bf16_pair(_load_slab_cols(buf_ref, tc, slab, c, row0=(half * k + r) * tc))
                acc_a = acc_a + gks[r] * ya
                acc_b = acc_b + gks[r] * yb
            o_ref[:, :, ca] = h_ref[:, :, ca] + gt_ref[:, :, ca] * acc_a.reshape(shp)
            o_ref[:, :, cb] = h_ref[:, :, cb] + gt_ref[:, :, cb] * acc_b.reshape(shp)

    @pl.when(i == 0)
    def _():
        fetch(slot_ref, 0)

    for half in range(2):
        @pl.when(i % 2 == half)
        def _(half=half):
            @pl.when(i + 1 < pl.num_programs(0))
            def _():
                fetch(slot_next_ref, 1 - half)

            wait(half)
            weighted_sum(half)


def _combine(h1, gate, ysh, ys, slots, gk, row0, slab):
    b, s, d = h1.shape
    rows = b * s
    tc = _tile(rows, 64)
    bt, st = _rows_block(b, s, tc)
    assert row0 % tc == 0
    t0 = row0 // tc
    rmap, bmap = _row_maps(bt, s // st)
    xmap = lambda i: (*rmap(i), 0)
    gmap = lambda i: (bmap(i), 0, 0)
    n_tok = gk.shape[0]
    nsteps = rows // tc
    slots3 = slots.reshape(n_tok // tc, 1, tc * TOP_K)
    return pl.pallas_call(
        functools.partial(_combine_kernel, slab=slab),
        grid=(nsteps,),
        in_specs=[pl.BlockSpec((1, 1, tc * TOP_K), lambda i: (t0 + i, 0, 0), memory_space=pltpu.SMEM),
                  pl.BlockSpec((1, 1, tc * TOP_K), lambda i: (t0 + jnp.minimum(i + 1, nsteps - 1), 0, 0),
                               memory_space=pltpu.SMEM),
                  pl.BlockSpec((bt, st, d), xmap),
                  pl.BlockSpec((bt, 1, d), gmap),
                  pl.BlockSpec((tc, d), lambda i: (t0 + i, 0)),
                  pl.BlockSpec((tc, TOP_K), lambda i: (t0 + i, 0)),
                  pl.BlockSpec(memory_space=pl.ANY)],
        out_specs=pl.BlockSpec((bt, st, d), xmap),
        out_shape=jax.ShapeDtypeStruct((b, s, d), F32),
        scratch_shapes=[pltpu.VMEM((2 * TOP_K * tc * slab, LANES), U32), pltpu.SemaphoreType.DMA((2,))],
        compiler_params=_cparams(1),
        name="combine",
    )(slots3, slots3, h1, gate, ysh, gk, ys)


def _slot_tables(counts, eidx, pos, tm):
    n_exp = counts.shape[0]
    k, n = eidx.shape
    nt_max = (k * n) // tm + n_exp
    tiles = (counts + tm - 1) // tm
    tile_end = jnp.cumsum(tiles)
    off = (tile_end - tiles) * tm
    n_tiles = tile_end[-1:].astype(I32)
    t_ar = jnp.arange(nt_max, dtype=I32)
    te = jnp.minimum(jnp.sum((tile_end[None, :] <= t_ar[:, None]).astype(I32), axis=1), n_exp - 1)
    te_last = jnp.sum(jnp.where(t_ar == n_tiles[0] - 1, te, 0))
    tile_expert = jnp.where(t_ar < n_tiles[0], te, te_last).astype(I32)
    onehot = eidx[None, :, :] == jnp.arange(n_exp, dtype=I32)[:, None, None]
    slot = pos + jnp.sum(jnp.where(onehot, off[:, None, None], 0), axis=0)
    pad_lo = (off + counts).astype(I32)
    pad_hi = jnp.concatenate([off[1:], jnp.full((1,), nt_max * tm, off.dtype)]).astype(I32)
    return slot.T.reshape(n * k).astype(I32), tile_expert, n_tiles, pad_lo, pad_hi, nt_max * tm


def kernel(x_prompt, x_sample, c_prompt, c_sample, cache_k, cache_v, cache_conv, w_ada, b_ada, g_mix, w_in, g_q, g_k, rel_bias, conv_w, conv_b, conv_ln_g, conv_ln_b, beta_attn, beta_conv, w_out, g_ffn, w_router, router_bias, w_gate, w_up, w_down, ws_gate, ws_up, ws_down):
    depth = w_in.shape[0]
    assert depth == 1
    bp, sp, d = x_prompt.shape
    bs, ss, _ = x_sample.shape
    head_dim = g_q.shape[1]
    n_heads = rel_bias.shape[1]
    d_attn = n_heads * head_dim
    d_conv = conv_w.shape[2]
    width = conv_w.shape[1]
    n_exp = w_router.shape[2]
    past_win = cache_k.shape[3]
    l = 0

    n_c = bp + bs
    c_all = jnp.concatenate([c_prompt, c_sample], axis=0)
    c_all = jnp.pad(c_all, ((0, -n_c % 16), (0, 0)))
    mod = _ada(c_all, w_ada[l], b_ada[l])
    mod_p = mod[:bp].reshape(bp, 1, 6 * d)
    mod_s = mod[bp:n_c].reshape(bs, 1, 6 * d)
    part = lambda m, i: m[:, :, i * d:(i + 1) * d]

    w_in_bf = w_in[l].astype(BF16)
    w_out_bf = w_out[l].astype(BF16)
    gh = jnp.concatenate([jnp.tile(g_q[l], n_heads), jnp.tile(g_k[l], n_heads)]).reshape(1, 2 * d_attn)
    tab_p, tab_c, tab_n = _bias_tables(rel_bias[l], LEFT_REACH // 2, ss, past_win)
    wr_t = w_router[l].T
    wr_hi = wr_t.astype(BF16)
    wr_lo = (wr_t - wr_hi.astype(F32)).astype(BF16)

    def mixer(x, m, attn_fn, prefix):
        b, s, _ = x.shape
        z = _inproj(x, part(m, 0), part(m, 1), g_mix[l], w_in_bf, gh, head_dim)
        z3 = z.reshape(b, s, z.shape[1])
        o_attn = attn_fn(z3)
        o_conv, tail = _conv(z3, prefix, conv_w[l], conv_b[l], conv_ln_g[l], conv_ln_b[l], beta_conv[l],
                             d_attn, d_conv)
        h1 = _outproj(o_attn.reshape(b * s, d_attn), o_conv.reshape(b * s, d_conv), beta_attn[l], w_out_bf,
                      x, part(m, 2))
        return h1, z3, tail

    pad_pre = CONV_PAD_ROWS - (width - 1)
    zero_prefix = jnp.zeros((bp, CONV_PAD_ROWS, d_conv), F32)
    samp_prefix = jnp.pad(cache_conv[l], ((0, 0), (pad_pre, 0), (0, 0)))
    h1_p, z3_p, tail_p = mixer(x_prompt, mod_p, lambda z3: _attn_prompt(z3, tab_p, n_heads, head_dim, d_attn),
                               zero_prefix)
    h1_s, z3_s, tail_s = mixer(x_sample, mod_s,
                               lambda z3: _attn_sample(z3, cache_k[l], cache_v[l], tab_c, tab_n, n_heads,
                                                       head_dim, d_attn), samp_prefix)

    slab = d // 2 // LANES
    xp, eidx, gk, pos, cnt = _router(h1_p, (part(mod_p, 3), part(mod_p, 4)), h1_s, (part(mod_s, 3), part(mod_s, 4)),
                                     g_ffn[l], wr_hi, wr_lo, router_bias[l])
    counts = cnt[:, 0].astype(I32)
    slots, tile_expert, n_tiles, pad_lo, pad_hi, n_slots = _slot_tables(counts, eidx, pos, EXPERT_TILE)
    ysh, xs = _shared_dispatch(xp, slots, pad_lo, pad_hi, n_slots, slab, ws_gate[l].astype(BF16),
                               ws_up[l].astype(BF16), ws_down[l].astype(BF16))
    ys = _expert_ffn(xs, tile_expert, n_tiles, w_gate[l], w_up[l], w_down[l], EXPERT_TILE, slab)
    gk_t = gk.T
    y_p = _combine(h1_p, part(mod_p, 5), ysh, ys, slots, gk_t, 0, slab)
    y_s = _combine(h1_s, part(mod_s, 5), ysh, ys, slots, gk_t, bp * sp, slab)

    def heads(z3, col0, keep):
        b, s, _ = z3.shape
        t = z3[:, s - keep:, col0:col0 + d_attn].astype(F32)
        return t.reshape(b, keep, n_heads, head_dim).transpose(0, 2, 1, 3)[None]

    keep_p = min(LEFT_REACH, sp)
    nw = width - 1
    return (y_p, y_s,
            heads(z3_p, d_attn, keep_p), heads(z3_p, 2 * d_attn, keep_p), tail_p[None, :, CONV_PAD_ROWS - nw:],
            heads(z3_s, d_attn, ss), heads(z3_s, 2 * d_attn, ss), tail_s[None, :, CONV_PAD_ROWS - nw:])
```

```python
import functools

import jax
import jax.numpy as jnp
from jax import lax
from jax.experimental import pallas as pl
from jax.experimental.pallas import tpu as pltpu

F32 = jnp.float32
BF16 = jnp.bfloat16
U32 = jnp.uint32
I32 = jnp.int32

CHUNK = 64
N_LEFT_CHUNKS = 8
LEFT_REACH = N_LEFT_CHUNKS * CHUNK
MAX_REL = 128
N_GROUPS = 8
TOPK_GROUPS = 4
TOP_K = 8
ROUTED_SCALE = 2.5
EPS = 1e-6
NEG_INF = -1e30

LANES = 128
SUBLANES = 8
CONV_PAD_ROWS = 32
V7X_VMEM_LIMIT = 56 * 1024 * 1024
EXPERT_TILE = 512


def _cparams(n_axes):
    return pltpu.CompilerParams(dimension_semantics=("arbitrary",) * n_axes,
                                vmem_limit_bytes=V7X_VMEM_LIMIT)


def _tile(n, pref):
    t = min(n, pref)
    while n % t:
        t //= 2
    return t


def _rows_block(batch, seq, rows):
    st = min(seq, rows)
    bt = rows // st
    assert bt * st == rows and batch % bt == 0 and seq % st == 0
    return bt, st


def _row_maps(bt, spb):
    if bt == 1:
        return (lambda i: (i // spb, i % spb)), (lambda i: i // spb)
    return (lambda i: (i, 0)), (lambda i: i)


def _pack_bf16_pair(a, b):
    au = lax.bitcast_convert_type(a.astype(BF16).astype(F32), U32)
    bu = lax.bitcast_convert_type(b.astype(BF16).astype(F32), U32)
    return au | (bu >> 16)


def _unpack_bf16_pair(p):
    a = lax.bitcast_convert_type(p & jnp.uint32(0xFFFF0000), F32)
    b = lax.bitcast_convert_type(p << 16, F32)
    return a, b


def _store_slabs(ref, val):
    n, w = val.shape
    slab = w // LANES
    for c in range(slab):
        ref[pl.ds(c, n, stride=slab), :] = val[:, c * LANES:(c + 1) * LANES]


def _load_slab_cols(ref, n, slab, c, row0=0):
    return ref[pl.ds(row0 * slab + c, n, stride=slab), :]


def _load_slabs(ref, n, slab):
    return jnp.concatenate([_load_slab_cols(ref, n, slab, c) for c in range(slab)], axis=1)


def _silu(x):
    return x * jax.nn.sigmoid(x)


def _ada_kernel(c_ref, w_ref, b_ref, o_ref):
    a = _silu(c_ref[...]).astype(BF16)
    o_ref[...] = jnp.dot(a, w_ref[...].astype(BF16), preferred_element_type=F32) + b_ref[...]


def _ada(c, w, b):
    m, d = c.shape
    n = w.shape[1]
    tn = _tile(n, 512)
    return pl.pallas_call(
        _ada_kernel,
        grid=(n // tn,),
        in_specs=[pl.BlockSpec((m, d), lambda j: (0, 0)),
                  pl.BlockSpec((d, tn), lambda j: (0, j)),
                  pl.BlockSpec((1, tn), lambda j: (0, j))],
        out_specs=pl.BlockSpec((m, tn), lambda j: (0, j)),
        out_shape=jax.ShapeDtypeStruct((m, n), F32),
        compiler_params=_cparams(1),
        name="ada",
    )(c, w, b.reshape(1, n))


def _modnorm(x, g, shift, scale):
    ms = jnp.mean(x * x, axis=-1, keepdims=True)
    y = x * lax.rsqrt(ms + EPS) * g
    return y * (1.0 + scale) + shift


def _inproj_kernel(x_ref, sh_ref, sc_ref, g_ref, w_ref, gh_ref, o_ref, hn_ref, *, n_norm_tiles, head_dim):
    j = pl.program_id(1)
    tm, d = hn_ref.shape
    tn = o_ref.shape[1]

    @pl.when(j == 0)
    def _():
        h = _modnorm(x_ref[...], g_ref[...], sh_ref[...], sc_ref[...])
        hn_ref[...] = h.reshape(tm, d).astype(BF16)

    acc = jnp.dot(hn_ref[...], w_ref[...], preferred_element_type=F32)

    @pl.when(j < n_norm_tiles)
    def _():
        for hh in range(tn // head_dim):
            sl = slice(hh * head_dim, (hh + 1) * head_dim)
            a = acc[:, sl]
            ms = jnp.mean(a * a, axis=-1, keepdims=True)
            o_ref[:, sl] = (a * lax.rsqrt(ms + EPS) * gh_ref[:, sl]).astype(BF16)

    @pl.when(j >= n_norm_tiles)
    def _():
        o_ref[...] = acc.astype(BF16)


def _inproj(x, shift, scale, g, w_bf, gh, head_dim):
    b, s, d = x.shape
    n = w_bf.shape[1]
    rows = b * s
    tm = _tile(rows, 512)
    tn = _tile(n, 1024)
    bt, st = _rows_block(b, s, tm)
    n_norm = gh.shape[1] // tn
    assert n_norm * tn == gh.shape[1] and tn % head_dim == 0
    rmap, bmap = _row_maps(bt, s // st)
    xmap = lambda i, j: (*rmap(i), 0)
    mmap = lambda i, j: (bmap(i), 0, 0)
    return pl.pallas_call(
        functools.partial(_inproj_kernel, n_norm_tiles=n_norm, head_dim=head_dim),
        grid=(rows // tm, n // tn),
        in_specs=[pl.BlockSpec((bt, st, d), xmap),
                  pl.BlockSpec((bt, 1, d), mmap),
                  pl.BlockSpec((bt, 1, d), mmap),
                  pl.BlockSpec((1, 1, d), lambda i, j: (0, 0, 0)),
                  pl.BlockSpec((d, tn), lambda i, j: (0, j)),
                  pl.BlockSpec((1, tn), lambda i, j: (0, jnp.minimum(j, n_norm - 1)))],
        out_specs=pl.BlockSpec((tm, tn), lambda i, j: (i, j)),
        out_shape=jax.ShapeDtypeStruct((rows, n), BF16),
        scratch_shapes=[pltpu.VMEM((tm, d), BF16)],
        compiler_params=_cparams(2),
        name="inproj",
    )(x, shift, scale, g.reshape(1, 1, d), w_bf, gh)


def _softmax_pv(s_list, v_list):
    m = functools.reduce(jnp.maximum, [jnp.max(s, axis=-1, keepdims=True) for s in s_list])
    p_list = [jnp.exp(s - m) for s in s_list]
    l = functools.reduce(jnp.add, [jnp.sum(p, axis=-1, keepdims=True) for p in p_list])
    o = functools.reduce(jnp.add, [jnp.dot(p.astype(BF16), v, preferred_element_type=F32)
                                   for p, v in zip(p_list, v_list)])
    return o / l


def _nt_dot(a, b):
    return lax.dot_general(a, b, (((1,), (1,)), ((), ())), preferred_element_type=F32)


def _attn_prompt_kernel(q_ref, k0_ref, k1_ref, k2_ref, v0_ref, v1_ref, v2_ref, t_ref, o_ref, *, scale, head_dim):
    i = pl.program_id(2)
    qb = q_ref.shape[1]
    for g in range(q_ref.shape[2] // head_dim):
        sl = slice(g * head_dim, (g + 1) * head_dim)
        q = q_ref[0, :, sl]
        k = jnp.concatenate([k0_ref[0, :, sl], k1_ref[0, :, sl], k2_ref[0, :, sl]], axis=0)
        v = jnp.concatenate([v0_ref[0, :, sl], v1_ref[0, :, sl], v2_ref[0, :, sl]], axis=0)
        s = _nt_dot(q, k) * scale + t_ref[g]
        col = lax.broadcasted_iota(I32, s.shape, 1)
        s = jnp.where(col < (2 - i) * qb, NEG_INF, s)
        o_ref[0, :, sl] = _softmax_pv([s], [v]).astype(BF16)


def _attn_prompt(z3, table, n_heads, head_dim, d_attn):
    b, s, _ = z3.shape
    qb = LEFT_REACH // 2
    hg = _tile(n_heads, 4)
    wg = hg * head_dim
    assert s % qb == 0 and table.shape == (n_heads, qb, 3 * qb)
    ck, cv = d_attn // wg, 2 * d_attn // wg

    def kv_spec(off, back):
        return pl.BlockSpec((1, qb, wg), lambda bb, h, i: (bb, jnp.maximum(i - back, 0), off + h))

    return pl.pallas_call(
        functools.partial(_attn_prompt_kernel, scale=head_dim ** -0.5, head_dim=head_dim),
        grid=(b, n_heads // hg, s // qb),
        in_specs=[pl.BlockSpec((1, qb, wg), lambda bb, h, i: (bb, i, h)),
                  kv_spec(ck, 2), kv_spec(ck, 1), kv_spec(ck, 0),
                  kv_spec(cv, 2), kv_spec(cv, 1), kv_spec(cv, 0),
                  pl.BlockSpec((hg, qb, 3 * qb), lambda bb, h, i: (h, 0, 0))],
        out_specs=pl.BlockSpec((1, qb, wg), lambda bb, h, i: (bb, i, h)),
        out_shape=jax.ShapeDtypeStruct((b, s, d_attn), BF16),
        compiler_params=_cparams(3),
        name="attn_prompt",
    )(z3, z3, z3, z3, z3, z3, z3, table)


def _attn_sample_kernel(q_ref, kn_ref, vn_ref, ck_ref, cv_ref, tc_ref, tn_ref, o_ref, *, scale, head_dim):
    for g in range(q_ref.shape[2] // head_dim):
        sl = slice(g * head_dim, (g + 1) * head_dim)
        q = q_ref[0, :, sl]
        ck = ck_ref[0, g].astype(BF16)
        cv = cv_ref[0, g].astype(BF16)
        s_c = _nt_dot(q, ck) * scale + tc_ref[g]
        s_n = _nt_dot(q, kn_ref[0, :, sl]) * scale + tn_ref[g]
        o_ref[0, :, sl] = _softmax_pv([s_c, s_n], [cv, vn_ref[0, :, sl]]).astype(BF16)


def _attn_sample(z3, cache_k, cache_v, tab_c, tab_n, n_heads, head_dim, d_attn):
    b, t, _ = z3.shape
    w = cache_k.shape[2]
    hg = _tile(n_heads, 8)
    wg = hg * head_dim
    ck, cv = d_attn // wg, 2 * d_attn // wg
    return pl.pallas_call(
        functools.partial(_attn_sample_kernel, scale=head_dim ** -0.5, head_dim=head_dim),
        grid=(b, n_heads // hg),
        in_specs=[pl.BlockSpec((1, t, wg), lambda bb, h: (bb, 0, h)),
                  pl.BlockSpec((1, t, wg), lambda bb, h: (bb, 0, ck + h)),
                  pl.BlockSpec((1, t, wg), lambda bb, h: (bb, 0, cv + h)),
                  pl.BlockSpec((1, hg, w, head_dim), lambda bb, h: (bb, h, 0, 0)),
                  pl.BlockSpec((1, hg, w, head_dim), lambda bb, h: (bb, h, 0, 0)),
                  pl.BlockSpec((hg, t, w), lambda bb, h: (h, 0, 0)),
                  pl.BlockSpec((hg, t, t), lambda bb, h: (h, 0, 0))],
        out_specs=pl.BlockSpec((1, t, wg), lambda bb, h: (bb, 0, h)),
        out_shape=jax.ShapeDtypeStruct((b, t, d_attn), BF16),
        compiler_params=_cparams(2),
        name="attn_sample",
    )(z3, z3, z3, cache_k, cache_v, tab_c, tab_n)


def _rel_bias_tile(rel_bias, n_rows, n_cols, delta0):
    col0 = n_rows - 1
    length = col0 + n_cols + 1
    idx = jnp.clip(jnp.arange(length) - col0 + delta0, -MAX_REL, MAX_REL) + MAX_REL
    ext = rel_bias[:, idx]
    h = ext.shape[0]
    skew = jnp.tile(ext, (1, n_rows))[:, :n_rows * (length - 1)].reshape(h, n_rows, length - 1)
    return skew[:, :, col0:col0 + n_cols].astype(F32)


def _bias_tables(rel_bias, qb, t_new, past_win):
    r = jnp.arange(qb)[:, None]
    j = jnp.arange(3 * qb)[None, :]
    dchunk = j // CHUNK - r // CHUNK
    ok = (dchunk >= 0) & (dchunk <= N_LEFT_CHUNKS)
    tab_p = jnp.where(ok[None], _rel_bias_tile(rel_bias, qb, 3 * qb, -2 * qb), NEG_INF)
    tab_c = _rel_bias_tile(rel_bias, t_new, past_win, -past_win)
    tab_n = _rel_bias_tile(rel_bias, t_new, t_new, 0)
    return tab_p, tab_c, tab_n


def _conv_kernel(a_ref, g_ref, pre_ref, cw_ref, cb_ref, lg_ref, lb_ref, beta_ref, o_ref, tail_ref, buf_ref, y_ref,
                 *, width):
    t = pl.program_id(1)
    ts = a_ref.shape[1]
    rc, c_all = y_ref.shape
    pad = CONV_PAD_ROWS

    @pl.when(t == 0)
    def _():
        buf_ref[0:pad, :] = pre_ref[0]

    @pl.when(t > 0)
    def _():
        buf_ref[0:pad, :] = buf_ref[ts:ts + pad, :]

    buf_ref[pad:pad + ts, :] = a_ref[0].astype(F32) * jax.nn.sigmoid(g_ref[0].astype(F32))
    tail_ref[0] = buf_ref[ts:ts + pad, :]
    first = pad - (width - 1)
    taps_by_shift = [[w for w in range(width) if (first + w) % SUBLANES == b] for b in range(SUBLANES)]

    def chunk(ci, carry):
        base = pl.multiple_of(ci * rc, rc)

        def colblock(cb, carry2):
            col = pl.multiple_of(cb * LANES, LANES)
            win = buf_ref[pl.ds(base, rc + pad), pl.ds(col, LANES)]
            acc = None
            for b, taps in enumerate(taps_by_shift):
                if not taps:
                    continue
                sh = win if b == 0 else pltpu.roll(win, rc + pad - b, axis=0)
                for w in taps:
                    a8 = (first + w) // SUBLANES * SUBLANES
                    term = sh[a8:a8 + rc, :] * cw_ref[pl.ds(w, 1), pl.ds(col, LANES)]
                    acc = term if acc is None else acc + term
            y_ref[:, pl.ds(col, LANES)] = acc
            return carry2

        lax.fori_loop(0, c_all // LANES, colblock, 0)
        y = y_ref[...] + cb_ref[...]
        mu = jnp.mean(y, axis=-1, keepdims=True)
        yc = y - mu
        var = jnp.mean(yc * yc, axis=-1, keepdims=True)
        s = _silu(yc * lax.rsqrt(var + EPS) * lg_ref[...] + lb_ref[...])
        ms = jnp.mean(s * s, axis=-1, keepdims=True)
        o_ref[0, pl.ds(base, rc), :] = (s * lax.rsqrt(ms + EPS) * beta_ref[...]).astype(BF16)
        return carry

    lax.fori_loop(0, ts // rc, chunk, 0)


def _conv(z3, prefix, conv_w, conv_b, ln_g, ln_b, beta, d_attn, d_conv):
    b, t, _ = z3.shape
    width = conv_w.shape[0]
    ts = _tile(t, 256)
    rc = _tile(ts, 64)
    assert 3 * d_attn % d_conv == 0 and rc % 16 == 0 and ts >= CONV_PAD_ROWS and width - 1 <= CONV_PAD_ROWS
    ca = 3 * d_attn // d_conv
    vec = lambda x: x.reshape(1, d_conv)
    cst = lambda shape: pl.BlockSpec(shape, lambda bb, tt: (0,) * len(shape))
    return pl.pallas_call(
        functools.partial(_conv_kernel, width=width),
        grid=(b, t // ts),
        in_specs=[pl.BlockSpec((1, ts, d_conv), lambda bb, tt: (bb, tt, ca)),
                  pl.BlockSpec((1, ts, d_conv), lambda bb, tt: (bb, tt, ca + 1)),
                  pl.BlockSpec((1, CONV_PAD_ROWS, d_conv), lambda bb, tt: (bb, 0, 0)),
                  cst((width, d_conv)), cst((1, d_conv)), cst((1, d_conv)), cst((1, d_conv)), cst((1, d_conv))],
        out_specs=[pl.BlockSpec((1, ts, d_conv), lambda bb, tt: (bb, tt, 0)),
                   pl.BlockSpec((1, CONV_PAD_ROWS, d_conv), lambda bb, tt: (bb, 0, 0))],
        out_shape=[jax.ShapeDtypeStruct((b, t, d_conv), BF16),
                   jax.ShapeDtypeStruct((b, CONV_PAD_ROWS, d_conv), F32)],
        scratch_shapes=[pltpu.VMEM((CONV_PAD_ROWS + ts, d_conv), F32), pltpu.VMEM((rc, d_conv), F32)],
        compiler_params=_cparams(2),
        name="conv",
    )(z3, z3, prefix, conv_w, vec(conv_b), vec(ln_g), vec(ln_b), vec(beta))


def _outproj_kernel(oa_ref, oc_ref, ba_ref, w_ref, x_ref, gt_ref, o_ref, mx_ref):
    j = pl.program_id(1)
    da = oa_ref.shape[1]

    @pl.when(j == 0)
    def _():
        oa = oa_ref[...].astype(F32)
        ms = jnp.mean(oa * oa, axis=-1, keepdims=True)
        mx_ref[:, :da] = (oa * lax.rsqrt(ms + EPS) * ba_ref[...]).astype(BF16)
        mx_ref[:, da:] = oc_ref[...]

    acc = jnp.dot(mx_ref[...], w_ref[...], preferred_element_type=F32)
    o_ref[...] = x_ref[...] + gt_ref[...] * acc.reshape(x_ref.shape)


def _outproj(o_attn, o_conv, beta_attn, w_bf, x, gate):
    b, s, d = x.shape
    rows = b * s
    da, dc = o_attn.shape[1], o_conv.shape[1]
    tm = _tile(rows, 512)
    tn = _tile(d, 1024)
    bt, st = _rows_block(b, s, tm)
    rmap, bmap = _row_maps(bt, s // st)
    xmap = lambda i, j: (*rmap(i), j)
    gmap = lambda i, j: (bmap(i), 0, j)
    return pl.pallas_call(
        _outproj_kernel,
        grid=(rows // tm, d // tn),
        in_specs=[pl.BlockSpec((tm, da), lambda i, j: (i, 0)),
                  pl.BlockSpec((tm, dc), lambda i, j: (i, 0)),
                  pl.BlockSpec((1, da), lambda i, j: (0, 0)),
                  pl.BlockSpec((da + dc, tn), lambda i, j: (0, j)),
                  pl.BlockSpec((bt, st, tn), xmap),
                  pl.BlockSpec((bt, 1, tn), gmap)],
        out_specs=pl.BlockSpec((bt, st, tn), xmap),
        out_shape=jax.ShapeDtypeStruct((b, s, d), F32),
        scratch_shapes=[pltpu.VMEM((tm, da + dc), BF16)],
        compiler_params=_cparams(2),
        name="outproj",
    )(o_attn, o_conv, beta_attn.reshape(1, da), w_bf, x, gate)


def _router_kernel(ha_ref, hb_ref, sha_ref, sca_ref, shb_ref, scb_ref, g_ref, wh_ref, wl_ref, rb_ref, tri_ref,
                   xp_ref, ei_ref, gk_ref, pos_ref, cnt_ref, hbuf_ref, run_ref, *, n_a_tiles):
    i = pl.program_id(0)
    tm, d = hbuf_ref.shape
    dh = d // 2
    n_exp = wh_ref.shape[0]
    gsz = n_exp // N_GROUPS

    @pl.when(i == 0)
    def _():
        run_ref[...] = jnp.zeros(run_ref.shape, F32)

    @pl.when(i < n_a_tiles)
    def _():
        hbuf_ref[...] = _modnorm(ha_ref[...], g_ref[...], sha_ref[...], sca_ref[...]).reshape(tm, d)

    @pl.when(i >= n_a_tiles)
    def _():
        hbuf_ref[...] = _modnorm(hb_ref[...], g_ref[...], shb_ref[...], scb_ref[...]).reshape(tm, d)

    h = hbuf_ref[...]
    _store_slabs(xp_ref, _pack_bf16_pair(h[:, :dh], h[:, dh:]))
    hi = h.astype(BF16)
    lo = (h - hi.astype(F32)).astype(BF16)
    logits = _nt_dot(wh_ref[...], hi) + (_nt_dot(wh_ref[...], lo) + _nt_dot(wl_ref[...], hi))
    scores = jax.nn.sigmoid(logits)
    choice = scores + rb_ref[...]

    sub = lax.broadcasted_iota(I32, (gsz, tm), 0).astype(F32)
    blocks, gs_rows = [], []
    for g in range(N_GROUPS):
        blk = choice[g * gsz:(g + 1) * gsz, :]
        m1 = jnp.max(blk, axis=0, keepdims=True)
        first = jnp.min(jnp.where(blk == m1, sub, float(gsz)), axis=0, keepdims=True)
        m2 = jnp.max(jnp.where(sub == first, -jnp.inf, blk), axis=0, keepdims=True)
        blocks.append(blk)
        gs_rows.append(m1 + m2)
    gs = jnp.concatenate(gs_rows, axis=0)

    def rank_desc(x):
        rows = lax.broadcasted_iota(I32, x.shape, 0)
        rank = jnp.zeros(x.shape, I32)
        for r2 in range(x.shape[0]):
            other = x[r2:r2 + 1, :]
            ahead = jnp.where(other > x, 1, jnp.where((other == x) & (rows > r2), 1, 0))
            rank = rank + ahead
        return rank, rows

    grank, _ = rank_desc(gs)
    masked = jnp.concatenate(
        [jnp.where(grank[g:g + 1, :] < TOPK_GROUPS, blocks[g], NEG_INF) for g in range(N_GROUPS)], axis=0)
    erank, erow = rank_desc(masked)
    sel = erank < TOP_K
    w = jnp.where(sel, scores, 0.0)
    gates = w / jnp.sum(w, axis=0, keepdims=True) * ROUTED_SCALE

    sel_f = jnp.where(sel, 1.0, 0.0)
    incl = jnp.dot(sel_f.astype(BF16), tri_ref[...], preferred_element_type=F32)
    pos_e = run_ref[:, 0:1] + (incl - sel_f)
    run_ref[...] = run_ref[...] + incl[:, tm - 1:tm]
    cnt_ref[...] = run_ref[...]

    erow_f = erow.astype(F32)
    ei_rows, gk_rows, pos_rows = [], [], []
    for r in range(TOP_K):
        hit = erank == r
        ei_rows.append(jnp.sum(jnp.where(hit, erow_f, 0.0), axis=0, keepdims=True))
        gk_rows.append(jnp.sum(jnp.where(hit, gates, 0.0), axis=0, keepdims=True))
        pos_rows.append(jnp.sum(jnp.where(hit, pos_e, 0.0), axis=0, keepdims=True))
    ei_ref[...] = jnp.concatenate(ei_rows, axis=0).astype(I32)
    gk_ref[...] = jnp.concatenate(gk_rows, axis=0)
    pos_ref[...] = jnp.concatenate(pos_rows, axis=0).astype(I32)


def _router(h_a, mod_a, h_b, mod_b, g, wr_hi, wr_lo, rbias):
    ba, sa, d = h_a.shape
    bb, sb, _ = h_b.shape
    rows_a, rows_b = ba * sa, bb * sb
    n_exp = wr_hi.shape[0]
    tm = min(_tile(rows_a, 256), _tile(rows_b, 256))
    assert rows_a % tm == 0 and rows_b % tm == 0 and (d // 2) % LANES == 0
    slab = d // 2 // LANES
    nta, ntb = rows_a // tm, rows_b // tm
    bta, sta = _rows_block(ba, sa, tm)
    btb, stb = _rows_block(bb, sb, tm)
    rmap_a, bmap_a = _row_maps(bta, sa // sta)
    rmap_b, bmap_b = _row_maps(btb, sb // stb)
    ia = lambda i: jnp.minimum(i, nta - 1)
    ib = lambda i: jnp.maximum(i - nta, 0)
    tri = (jnp.arange(tm)[:, None] <= jnp.arange(tm)[None, :]).astype(BF16)
    n_tok = rows_a + rows_b
    cst2 = lambda shape: pl.BlockSpec(shape, lambda i: (0, 0))
    return pl.pallas_call(
        functools.partial(_router_kernel, n_a_tiles=nta),
        grid=(nta + ntb,),
        in_specs=[pl.BlockSpec((bta, sta, d), lambda i: (*rmap_a(ia(i)), 0)),
                  pl.BlockSpec((btb, stb, d), lambda i: (*rmap_b(ib(i)), 0)),
                  pl.BlockSpec((bta, 1, d), lambda i: (bmap_a(ia(i)), 0, 0)),
                  pl.BlockSpec((bta, 1, d), lambda i: (bmap_a(ia(i)), 0, 0)),
                  pl.BlockSpec((btb, 1, d), lambda i: (bmap_b(ib(i)), 0, 0)),
                  pl.BlockSpec((btb, 1, d), lambda i: (bmap_b(ib(i)), 0, 0)),
                  pl.BlockSpec((1, 1, d), lambda i: (0, 0, 0)),
                  cst2((n_exp, d)), cst2((n_exp, d)), cst2((n_exp, 1)), cst2((tm, tm))],
        out_specs=[pl.BlockSpec((tm * slab, LANES), lambda i: (i, 0)),
                   pl.BlockSpec((TOP_K, tm), lambda i: (0, i)),
                   pl.BlockSpec((TOP_K, tm), lambda i: (0, i)),
                   pl.BlockSpec((TOP_K, tm), lambda i: (0, i)),
                   cst2((n_exp, LANES))],
        out_shape=[jax.ShapeDtypeStruct((n_tok * slab, LANES), U32),
                   jax.ShapeDtypeStruct((TOP_K, n_tok), I32),
                   jax.ShapeDtypeStruct((TOP_K, n_tok), F32),
                   jax.ShapeDtypeStruct((TOP_K, n_tok), I32),
                   jax.ShapeDtypeStruct((n_exp, LANES), F32)],
        scratch_shapes=[pltpu.VMEM((tm, d), F32), pltpu.VMEM((n_exp, LANES), F32)],
        compiler_params=_cparams(1),
        name="router",
    )(h_a, h_b, mod_a[0], mod_a[1], mod_b[0], mod_b[1], g.reshape(1, 1, d), wr_hi, wr_lo,
      rbias.reshape(n_exp, 1), tri)


def _dispatch_kernel(lo_ref, hi_ref, slot_ref, x_ref, xs_ref, zero_ref, sem, *, n_tok_tiles, slab):
    i = pl.program_id(0)
    n_dma = slot_ref.shape[2]
    k = TOP_K
    tc = n_dma // k

    def slot_copy(src, slot):
        return pltpu.make_async_copy(src, xs_ref.at[pl.ds(pl.multiple_of(slot * slab, slab), slab)], sem)

    @pl.when(i < n_tok_tiles)
    def _():
        def start(n, carry):
            src = x_ref.at[pl.ds(pl.multiple_of(n * slab, slab), slab)]
            for r in range(k):
                slot_copy(src, slot_ref[0, 0, n * k + r]).start(priority=r % 2)
            return carry

        def wait(n, carry):
            for r in range(k):
                slot_copy(x_ref.at[pl.ds(0, slab)], 0).wait()
            return carry

        lax.fori_loop(0, tc, start, 0)
        lax.fori_loop(0, tc, wait, 0)

    @pl.when(i >= n_tok_tiles)
    def _():
        e = i - n_tok_tiles
        zero_ref[...] = jnp.zeros(zero_ref.shape, zero_ref.dtype)

        def start(s, carry):
            slot_copy(zero_ref, s).start()
            return carry

        def wait(s, carry):
            slot_copy(zero_ref, 0).wait()
            return carry

        lax.fori_loop(lo_ref[e], hi_ref[e], start, 0)
        lax.fori_loop(lo_ref[e], hi_ref[e], wait, 0)


def _dispatch(xp, slots, pad_lo, pad_hi, n_slots, slab):
    n_tok = xp.shape[0] // slab
    n_exp = pad_lo.shape[0]
    tc = _tile(n_tok, 128)
    ntt = n_tok // tc
    return pl.pallas_call(
        functools.partial(_dispatch_kernel, n_tok_tiles=ntt, slab=slab),
        grid_spec=pltpu.PrefetchScalarGridSpec(
            num_scalar_prefetch=2,
            grid=(ntt + n_exp,),
            in_specs=[pl.BlockSpec((1, 1, tc * TOP_K), lambda i, lo, hi: (jnp.minimum(i, ntt - 1), 0, 0),
                                   memory_space=pltpu.SMEM),
                      pl.BlockSpec((tc * slab, LANES), lambda i, lo, hi: (jnp.minimum(i, ntt - 1), 0))],
            out_specs=pl.BlockSpec(memory_space=pl.ANY),
            scratch_shapes=[pltpu.VMEM((slab, LANES), U32), pltpu.SemaphoreType.DMA(())]),
        out_shape=jax.ShapeDtypeStruct((n_slots * slab, LANES), U32),
        compiler_params=_cparams(1),
        name="dispatch",
    )(pad_lo, pad_hi, slots.reshape(ntt, 1, tc * TOP_K), xp)


def _expert_changed(te_ref, t):
    return (t == 0) | (te_ref[t] != te_ref[jnp.maximum(t - 1, 0)])


def _gateup_kernel(te_ref, nt_ref, x_ref, wg_ref, wu_ref, h_ref, wgb_ref, wub_ref, *, slab):
    t = pl.program_id(1)
    tm = h_ref.shape[0]
    dh = slab * LANES

    @pl.when(t < nt_ref[0])
    def _():
        @pl.when(_expert_changed(te_ref, t))
        def _():
            wgb_ref[...] = wg_ref[0].astype(BF16)
            wub_ref[...] = wu_ref[0].astype(BF16)

        xa, xb = _unpack_bf16_pair(_load_slabs(x_ref, tm, slab))
        xa, xb = xa.astype(BF16), xb.astype(BF16)
        g = (jnp.dot(xa, wgb_ref[:dh, :], preferred_element_type=F32)
             + jnp.dot(xb, wgb_ref[dh:, :], preferred_element_type=F32))
        u = (jnp.dot(xa, wub_ref[:dh, :], preferred_element_type=F32)
             + jnp.dot(xb, wub_ref[dh:, :], preferred_element_type=F32))
        h_ref[...] = (_silu(g) * u).astype(BF16)

    @pl.when(t >= nt_ref[0])
    def _():
        h_ref[...] = jnp.zeros(h_ref.shape, h_ref.dtype)


def _down_kernel(te_ref, nt_ref, h_ref, wd_ref, y_ref, wdb_ref):
    t = pl.program_id(0)
    dh = wd_ref.shape[2] // 2

    @pl.when(t < nt_ref[0])
    def _():
        @pl.when(_expert_changed(te_ref, t))
        def _():
            wdb_ref[...] = wd_ref[0].astype(BF16)

        y = jnp.dot(h_ref[...], wdb_ref[...], preferred_element_type=F32)
        _store_slabs(y_ref, _pack_bf16_pair(y[:, :dh], y[:, dh:]))

    @pl.when(t >= nt_ref[0])
    def _():
        y_ref[...] = jnp.zeros(y_ref.shape, y_ref.dtype)


def _expert_ffn(xs, tile_expert, n_tiles, w_gate, w_up, w_down, tm, slab):
    n_exp, d, f = w_gate.shape
    s_rows = xs.shape[0] // slab
    nt = s_rows // tm
    tf = _tile(f, 512)
    last = lambda t, ntr: jnp.minimum(t, ntr[0] - 1)
    hmid = pl.pallas_call(
        functools.partial(_gateup_kernel, slab=slab),
        grid_spec=pltpu.PrefetchScalarGridSpec(
            num_scalar_prefetch=2,
            grid=(f // tf, nt),
            in_specs=[pl.BlockSpec((tm * slab, LANES), lambda j, t, te, ntr: (last(t, ntr), 0)),
                      pl.BlockSpec((1, d, tf), lambda j, t, te, ntr: (te[t], 0, j)),
                      pl.BlockSpec((1, d, tf), lambda j, t, te, ntr: (te[t], 0, j))],
            out_specs=pl.BlockSpec((tm, tf), lambda j, t, te, ntr: (t, j)),
            scratch_shapes=[pltpu.VMEM((d, tf), BF16), pltpu.VMEM((d, tf), BF16)]),
        out_shape=jax.ShapeDtypeStruct((s_rows, f), BF16),
        compiler_params=_cparams(2),
        name="expert_gateup",
    )(tile_expert, n_tiles, xs, w_gate, w_up)
    return pl.pallas_call(
        _down_kernel,
        grid_spec=pltpu.PrefetchScalarGridSpec(
            num_scalar_prefetch=2,
            grid=(nt,),
            in_specs=[pl.BlockSpec((tm, f), lambda t, te, ntr: (last(t, ntr), 0)),
                      pl.BlockSpec((1, f, d), lambda t, te, ntr: (te[t], 0, 0))],
            out_specs=pl.BlockSpec((tm * slab, LANES), lambda t, te, ntr: (t, 0)),
            scratch_shapes=[pltpu.VMEM((f, d), BF16)]),
        out_shape=jax.ShapeDtypeStruct((s_rows * slab, LANES), U32),
        compiler_params=_cparams(1),
        name="expert_down",
    )(tile_expert, n_tiles, hmid, w_down)


def _shared_kernel(x_ref, wg_ref, wu_ref, wd_ref, o_ref, xa_ref, xb_ref, acc_ref, *, slab):
    j = pl.program_id(1)
    tm, dh = xa_ref.shape

    @pl.when(j == 0)
    def _():
        xa, xb = _unpack_bf16_pair(_load_slabs(x_ref, tm, slab))
        xa_ref[...] = xa.astype(BF16)
        xb_ref[...] = xb.astype(BF16)

    xa, xb = xa_ref[...], xb_ref[...]
    g = (jnp.dot(xa, wg_ref[:dh, :], preferred_element_type=F32)
         + jnp.dot(xb, wg_ref[dh:, :], preferred_element_type=F32))
    u = (jnp.dot(xa, wu_ref[:dh, :], preferred_element_type=F32)
         + jnp.dot(xb, wu_ref[dh:, :], preferred_element_type=F32))
    part = jnp.dot((_silu(g) * u).astype(BF16), wd_ref[...], preferred_element_type=F32)

    @pl.when(j == 0)
    def _():
        acc_ref[...] = part

    @pl.when(j > 0)
    def _():
        acc_ref[...] += part

    @pl.when(j == pl.num_programs(1) - 1)
    def _():
        o_ref[...] = acc_ref[...].astype(BF16)


def _shared_ffn(xp, wg_bf, wu_bf, wd_bf, slab):
    d, f = wg_bf.shape
    rows = xp.shape[0] // slab
    tm = _tile(rows, 512)
    tf = _tile(f, 256)
    return pl.pallas_call(
        functools.partial(_shared_kernel, slab=slab),
        grid=(rows // tm, f // tf),
        in_specs=[pl.BlockSpec((tm * slab, LANES), lambda i, j: (i, 0)),
                  pl.BlockSpec((d, tf), lambda i, j: (0, j)),
                  pl.BlockSpec((d, tf), lambda i, j: (0, j)),
                  pl.BlockSpec((tf, d), lambda i, j: (j, 0))],
        out_specs=pl.BlockSpec((tm, d), lambda i, j: (i, 0)),
        out_shape=jax.ShapeDtypeStruct((rows, d), BF16),
        scratch_shapes=[pltpu.VMEM((tm, d // 2), BF16), pltpu.VMEM((tm, d // 2), BF16), pltpu.VMEM((tm, d), F32)],
        compiler_params=_cparams(2),
        name="shared_ffn",
    )(xp, wg_bf, wu_bf, wd_bf)


def _combine_kernel(slot_ref, slot_next_ref, h_ref, gt_ref, ys_ref, gk_ref, y_hbm, o_ref, buf_ref, sems, *, slab):
    i = pl.program_id(0)
    tc, d = ys_ref.shape
    dh = d // 2
    k = TOP_K

    def slab_copy(slot, half, n, r):
        dst_row = ((half * k + r) * tc + n) * slab
        return pltpu.make_async_copy(y_hbm.at[pl.ds(pl.multiple_of(slot * slab, slab), slab)],
                                     buf_ref.at[pl.ds(pl.multiple_of(dst_row, slab), slab)], sems.at[half])

    def fetch(sref, half):
        def start(n, carry):
            for r in range(k):
                slab_copy(sref[0, 0, n * k + r], half, n, r).start(priority=r % 2)
            return carry

        lax.fori_loop(0, tc, start, 0)

    def wait(half):
        def body(n, carry):
            for r in range(k):
                slab_copy(0, half, 0, 0).wait()
            return carry

        lax.fori_loop(0, tc, body, 0)

    def weighted_sum(half):
        shp = (h_ref.shape[0], h_ref.shape[1], LANES)
        gks = [gk_ref[:, r:r + 1] for r in range(k)]
        for c in range(slab):
            ca = slice(c * LANES, (c + 1) * LANES)
            cb = slice(dh + c * LANES, dh + (c + 1) * LANES)
            acc_a = ys_ref[:, ca].astype(F32)
            acc_b = ys_ref[:, cb].astype(F32)
            for r in range(k):
                ya, yb = _unpack_bf16_pair(_load_slab_cols(buf_ref, tc, slab, c, row0=(half * k + r) * tc))
                acc_a = acc_a + gks[r] * ya
                acc_b = acc_b + gks[r] * yb
            o_ref[:, :, ca] = h_ref[:, :, ca] + gt_ref[:, :, ca] * acc_a.reshape(shp)
            o_ref[:, :, cb] = h_ref[:, :, cb] + gt_ref[:, :, cb] * acc_b.reshape(shp)

    @pl.when(i == 0)
    def _():
        fetch(slot_ref, 0)

    for half in range(2):
        @pl.when(i % 2 == half)
        def _(half=half):
            @pl.when(i + 1 < pl.num_programs(0))
            def _():
                fetch(slot_next_ref, 1 - half)

            wait(half)
            weighted_sum(half)


def _combine(h1, gate, ysh, ys, slots, gk, row0, slab):
    b, s, d = h1.shape
    rows = b * s
    tc = _tile(rows, 64)
    bt, st = _rows_block(b, s, tc)
    assert row0 % tc == 0
    t0 = row0 // tc
    rmap, bmap = _row_maps(bt, s // st)
    xmap = lambda i: (*rmap(i), 0)
    gmap = lambda i: (bmap(i), 0, 0)
    n_tok = gk.shape[0]
    nsteps = rows // tc
    slots3 = slots.reshape(n_tok // tc, 1, tc * TOP_K)
    return pl.pallas_call(
        functools.partial(_combine_kernel, slab=slab),
        grid=(nsteps,),
        in_specs=[pl.BlockSpec((1, 1, tc * TOP_K), lambda i: (t0 + i, 0, 0), memory_space=pltpu.SMEM),
                  pl.BlockSpec((1, 1, tc * TOP_K), lambda i: (t0 + jnp.minimum(i + 1, nsteps - 1), 0, 0),
                               memory_space=pltpu.SMEM),
                  pl.BlockSpec((bt, st, d), xmap),
                  pl.BlockSpec((bt, 1, d), gmap),
                  pl.BlockSpec((tc, d), lambda i: (t0 + i, 0)),
                  pl.BlockSpec((tc, TOP_K), lambda i: (t0 + i, 0)),
                  pl.BlockSpec(memory_space=pl.ANY)],
        out_specs=pl.BlockSpec((bt, st, d), xmap),
        out_shape=jax.ShapeDtypeStruct((b, s, d), F32),
        scratch_shapes=[pltpu.VMEM((2 * TOP_K * tc * slab, LANES), U32), pltpu.SemaphoreType.DMA((2,))],
        compiler_params=_cparams(1),
        name="combine",
    )(slots3, slots3, h1, gate, ysh, gk, ys)


def _slot_tables(counts, eidx, pos, tm):
    n_exp = counts.shape[0]
    k, n = eidx.shape
    nt_max = (k * n) // tm + n_exp
    tiles = (counts + tm - 1) // tm
    tile_end = jnp.cumsum(tiles)
    off = (tile_end - tiles) * tm
    n_tiles = tile_end[-1:].astype(I32)
    t_ar = jnp.arange(nt_max, dtype=I32)
    te = jnp.minimum(jnp.sum((tile_end[None, :] <= t_ar[:, None]).astype(I32), axis=1), n_exp - 1)
    te_last = jnp.sum(jnp.where(t_ar == n_tiles[0] - 1, te, 0))
    tile_expert = jnp.where(t_ar < n_tiles[0], te, te_last).astype(I32)
    onehot = eidx[None, :, :] == jnp.arange(n_exp, dtype=I32)[:, None, None]
    slot = pos + jnp.sum(jnp.where(onehot, off[:, None, None], 0), axis=0)
    pad_lo = (off + counts).astype(I32)
    pad_hi = jnp.concatenate([off[1:], jnp.full((1,), nt_max * tm, off.dtype)]).astype(I32)
    return slot.T.reshape(n * k).astype(I32), tile_expert, n_tiles, pad_lo, pad_hi, nt_max * tm


def kernel(x_prompt, x_sample, c_prompt, c_sample, cache_k, cache_v, cache_conv, w_ada, b_ada, g_mix, w_in, g_q, g_k, rel_bias, conv_w, conv_b, conv_ln_g, conv_ln_b, beta_attn, beta_conv, w_out, g_ffn, w_router, router_bias, w_gate, w_up, w_down, ws_gate, ws_up, ws_down):
    depth = w_in.shape[0]
    assert depth == 1
    bp, sp, d = x_prompt.shape
    bs, ss, _ = x_sample.shape
    head_dim = g_q.shape[1]
    n_heads = rel_bias.shape[1]
    d_attn = n_heads * head_dim
    d_conv = conv_w.shape[2]
    width = conv_w.shape[1]
    n_exp = w_router.shape[2]
    past_win = cache_k.shape[3]
    l = 0

    n_c = bp + bs
    c_all = jnp.concatenate([c_prompt, c_sample], axis=0)
    c_all = jnp.pad(c_all, ((0, -n_c % 16), (0, 0)))
    mod = _ada(c_all, w_ada[l], b_ada[l])
    mod_p = mod[:bp].reshape(bp, 1, 6 * d)
    mod_s = mod[bp:n_c].reshape(bs, 1, 6 * d)
    part = lambda m, i: m[:, :, i * d:(i + 1) * d]

    w_in_bf = w_in[l].astype(BF16)
    w_out_bf = w_out[l].astype(BF16)
    gh = jnp.concatenate([jnp.tile(g_q[l], n_heads), jnp.tile(g_k[l], n_heads)]).reshape(1, 2 * d_attn)
    tab_p, tab_c, tab_n = _bias_tables(rel_bias[l], LEFT_REACH // 2, ss, past_win)
    wr_t = w_router[l].T
    wr_hi = wr_t.astype(BF16)
    wr_lo = (wr_t - wr_hi.astype(F32)).astype(BF16)

    def mixer(x, m, attn_fn, prefix):
        b, s, _ = x.shape
        z = _inproj(x, part(m, 0), part(m, 1), g_mix[l], w_in_bf, gh, head_dim)
        z3 = z.reshape(b, s, z.shape[1])
        o_attn = attn_fn(z3)
        o_conv, tail = _conv(z3, prefix, conv_w[l], conv_b[l], conv_ln_g[l], conv_ln_b[l], beta_conv[l],
                             d_attn, d_conv)
        h1 = _outproj(o_attn.reshape(b * s, d_attn), o_conv.reshape(b * s, d_conv), beta_attn[l], w_out_bf,
                      x, part(m, 2))
        return h1, z3, tail

    pad_pre = CONV_PAD_ROWS - (width - 1)
    zero_prefix = jnp.zeros((bp, CONV_PAD_ROWS, d_conv), F32)
    samp_prefix = jnp.pad(cache_conv[l], ((0, 0), (pad_pre, 0), (0, 0)))
    h1_p, z3_p, tail_p = mixer(x_prompt, mod_p, lambda z3: _attn_prompt(z3, tab_p, n_heads, head_dim, d_attn),
                               zero_prefix)
    h1_s, z3_s, tail_s = mixer(x_sample, mod_s,
                               lambda z3: _attn_sample(z3, cache_k[l], cache_v[l], tab_c, tab_n, n_heads,
                                                       head_dim, d_attn), samp_prefix)

    slab = d // 2 // LANES
    xp, eidx, gk, pos, cnt = _router(h1_p, (part(mod_p, 3), part(mod_p, 4)), h1_s, (part(mod_s, 3), part(mod_s, 4)),
                                     g_ffn[l], wr_hi, wr_lo, router_bias[l])
    counts = cnt[:, 0].astype(I32)
    slots, tile_expert, n_tiles, pad_lo, pad_hi, n_slots = _slot_tables(counts, eidx, pos, EXPERT_TILE)
    xs = _dispatch(xp, slots, pad_lo, pad_hi, n_slots, slab)
    ys = _expert_ffn(xs, tile_expert, n_tiles, w_gate[l], w_up[l], w_down[l], EXPERT_TILE, slab)
    ysh = _shared_ffn(xp, ws_gate[l].astype(BF16), ws_up[l].astype(BF16), ws_down[l].astype(BF16), slab)
    gk_t = gk.T
    y_p = _combine(h1_p, part(mod_p, 5), ysh, ys, slots, gk_t, 0, slab)
    y_s = _combine(h1_s, part(mod_s, 5), ysh, ys, slots, gk_t, bp * sp, slab)

    def heads(z3, col0, keep):
        b, s, _ = z3.shape
        t = z3[:, s - keep:, col0:col0 + d_attn].astype(F32)
        return t.reshape(b, keep, n_heads, head_dim).transpose(0, 2, 1, 3)[None]

    keep_p = min(LEFT_REACH, sp)
    nw = width - 1
    return (y_p, y_s,
            heads(z3_p, d_attn, keep_p), heads(z3_p, 2 * d_attn, keep_p), tail_p[None, :, CONV_PAD_ROWS - nw:],
            heads(z3_s, d_attn, ss), heads(z3_s, 2 * d_attn, ss), tail_s[None, :, CONV_PAD_ROWS - nw:])
```
